```python
import jax, jax.numpy as jnp
from jax import lax
import numpy as np

D_MODEL = 2048
BATCH = 1
SEQ = 8192
DEPTH = 4
DEC_BATCH = 16
DEC_SEQ = 64
PAST_LEN = 1024

CHUNK = 64
HEAD_DIM = 64
QBLK = 128
ROPE_THETA = 10000.0
EPS = 1e-5
NEG = -1e30
H_FOX = 16
H_DSA = 16
KV_DSA = 4
H_IDX = 8
D_IDX = 64
TOPK_MAX = 256
H_SWA = 32
KV_SWA = 4
WINDOW = 128
W_CHUNKS = WINDOW // CHUNK
N_EXPERTS = 32
TOP_K = 4
D_FF = D_MODEL
SWIGLU_ALPHA = 1.702
SWIGLU_LIMIT = 7.0
EXPERT_BLOCK = 128
N_AB = (DEPTH + 1) // 2
N_C = DEPTH // 2
AB_SIZES = (H_FOX * HEAD_DIM, H_FOX * HEAD_DIM, H_FOX * HEAD_DIM, H_FOX,
            H_DSA * HEAD_DIM, KV_DSA * HEAD_DIM, KV_DSA * HEAD_DIM,
            H_IDX * D_IDX, D_IDX, H_IDX)
D_IN_AB = 3 * H_FOX * HEAD_DIM + H_FOX + (H_DSA + 2 * KV_DSA) * HEAD_DIM + H_IDX * D_IDX + D_IDX + H_IDX
D_MIX_AB = (H_FOX + H_DSA) * HEAD_DIM
C_SIZES = (H_SWA * HEAD_DIM, KV_SWA * HEAD_DIM, KV_SWA * HEAD_DIM)
D_IN_C = (H_SWA + 2 * KV_SWA) * HEAD_DIM

kernel_name = 'hybrid_fox_dsa_swa_moe_stream_step'


def _rms(x, g):
    xf = x.astype(jnp.float32)
    return (xf * lax.rsqrt(jnp.mean(xf * xf, axis=-1, keepdims=True) + EPS)).astype(x.dtype) * g


def _split(z, sizes):
    cuts, acc = [], 0
    for s in sizes[:-1]:
        acc += s
        cuts.append(acc)
    return jnp.split(z, cuts, axis=-1)


def _rope(x, pos):
    half = x.shape[-1] // 2
    inv = ROPE_THETA ** (-jnp.arange(half, dtype=jnp.float32) / half)
    ang = pos.astype(jnp.float32)[:, None, None] * inv
    cos, sin = jnp.cos(ang), jnp.sin(ang)
    xf = x.astype(jnp.float32)
    x1, x2 = xf[..., :half], xf[..., half:]
    return jnp.concatenate([x1 * cos - x2 * sin, x2 * cos + x1 * sin], axis=-1).astype(x.dtype)


def _query_blocks(fn, *q_args):
    T = q_args[0].shape[1]
    if T <= QBLK:
        return fn(*q_args)
    nb = T // QBLK
    def to_blocks(a):
        return jnp.moveaxis(a.reshape(a.shape[0], nb, QBLK, *a.shape[2:]), 1, 0)
    out = lax.map(lambda args: fn(*args), tuple(to_blocks(a) for a in q_args))
    out = jnp.moveaxis(out, 0, 1)
    return out.reshape(out.shape[0], T, *out.shape[3:])


def _fox_attend(q, cq, qpos, k, v, ck, kpos):
    s = jnp.einsum('bqhd,bshd->bhqs', q, k).astype(jnp.float32) * HEAD_DIM ** -0.5
    s = s + jnp.swapaxes(cq, 1, 2)[..., :, None] - jnp.swapaxes(ck, 1, 2)[..., None, :]
    mask = kpos[None, None, None, :] <= qpos[:, None, :, None]
    p = jax.nn.softmax(jnp.where(mask, s, NEG), axis=-1).astype(v.dtype)
    return jnp.einsum('bhqs,bshd->bqhd', p, v)


def _dsa_attend(q, qi, wi, qpos, k, v, ki, kpos, topk):
    B, Tq = q.shape[:2]
    rel = jax.nn.relu(jnp.einsum('bqhd,bsd->bqhs', qi, ki).astype(jnp.float32) * D_IDX ** -0.5)
    score = jnp.einsum('bqh,bqhs->bqs', wi.astype(jnp.float32), rel)
    admissible = (kpos[None, None, :] // CHUNK) <= (qpos[:, :, None] // CHUNK)
    _, sel = lax.top_k(jnp.where(admissible, score, NEG), topk)
    valid = (kpos[sel] // CHUNK) <= (qpos[..., None] // CHUNK)
    ks = jax.vmap(lambda a, i: a[i])(k, sel)
    vs = jax.vmap(lambda a, i: a[i])(v, sel)
    qg = q.reshape(B, Tq, KV_DSA, H_DSA // KV_DSA, HEAD_DIM)
    s = jnp.einsum('bqkgd,bqskd->bqkgs', qg, ks).astype(jnp.float32) * HEAD_DIM ** -0.5
    p = jax.nn.softmax(jnp.where(valid[:, :, None, None, :], s, NEG), axis=-1).astype(vs.dtype)
    o = jnp.einsum('bqkgs,bqskd->bqkgd', p, vs)
    return o.reshape(B, Tq, H_DSA * HEAD_DIM)


def _band_valid(qpos, kpos):
    qc = qpos[..., :, None] // CHUNK
    kc = kpos[..., None, :] // CHUNK
    return (kpos[..., None, :] >= 0) & (kc <= qc) & (kc >= qc - W_CHUNKS)


def _sink_attend(q, k, v, valid, sinks):
    G = q.shape[4]
    s = jnp.einsum('bnqkgd,bnskd->bnkgqs', q, k).astype(jnp.float32) * HEAD_DIM ** -0.5
    s = jnp.where(valid[None, :, None, None], s, NEG)
    sink = sinks.astype(jnp.float32).reshape(1, 1, KV_SWA, G, 1, 1)
    m = jnp.maximum(jnp.max(s, axis=-1, keepdims=True), sink)
    e = jnp.exp(s - m)
    p = e / (jnp.sum(e, axis=-1, keepdims=True) + jnp.exp(sink - m))
    return jnp.einsum('bnkgqs,bnskd->bnqkgd', p.astype(v.dtype), v)


def _mixer_ab(h, pos, w_in, b_forget, w_out, past):
    B, T, _ = h.shape
    qa, ka, va, fa, qb, kb, vb, qi, ki, wi = _split(h @ w_in, AB_SIZES)
    qa = qa.reshape(B, T, H_FOX, HEAD_DIM)
    ka = ka.reshape(B, T, H_FOX, HEAD_DIM)
    va = va.reshape(B, T, H_FOX, HEAD_DIM)
    logf = jax.nn.log_sigmoid((fa + b_forget).astype(jnp.float32))
    qb = _rope(qb.reshape(B, T, H_DSA, HEAD_DIM), pos)
    kb = _rope(kb.reshape(B, T, KV_DSA, HEAD_DIM), pos)
    vb = vb.reshape(B, T, KV_DSA, HEAD_DIM)
    qi = _rope(qi.reshape(B, T, H_IDX, D_IDX), pos)
    ki = _rope(ki[:, :, None, :], pos)[:, :, 0]
    wi = wi * H_IDX ** -0.5
    fox_kv = jnp.stack([ka, va], axis=2)
    dsa_kv = jnp.stack([kb, vb], axis=2)
    if past is None:
        fk, fv, flog, dk, dv, dki = ka, va, logf, kb, vb, ki
    else:
        p_fox_kv, p_logf, p_dsa_kv, p_kidx = past
        fk = jnp.concatenate([p_fox_kv[:, :, 0], ka], axis=1)
        fv = jnp.concatenate([p_fox_kv[:, :, 1], va], axis=1)
        flog = jnp.concatenate([p_logf.astype(jnp.float32), logf], axis=1)
        dk = jnp.concatenate([p_dsa_kv[:, :, 0], kb], axis=1)
        dv = jnp.concatenate([p_dsa_kv[:, :, 1], vb], axis=1)
        dki = jnp.concatenate([p_kidx, ki], axis=1)
    S = fk.shape[1]
    kpos = jnp.arange(S, dtype=jnp.int32)
    cum = jnp.cumsum(flog, axis=1)
    qpos = jnp.broadcast_to(pos, (B, T))
    oa = _query_blocks(lambda q_, c_, p_: _fox_attend(q_, c_, p_, fk, fv, cum, kpos), qa, cum[:, S - T:], qpos)
    topk = min(TOPK_MAX, S // 4)
    ob = _query_blocks(lambda q_, qi_, wi_, p_: _dsa_attend(q_, qi_, wi_, p_, dk, dv, dki, kpos, topk), qb, qi, wi, qpos)
    o = jnp.concatenate([oa.reshape(B, T, H_FOX * HEAD_DIM), ob], axis=-1) @ w_out
    return o, (fox_kv, logf.astype(h.dtype), dsa_kv, ki)


def _mixer_c(h, pos, w_in, sinks, w_out, past):
    B, T, _ = h.shape
    G = H_SWA // KV_SWA
    q, k, v = _split(h @ w_in, C_SIZES)
    q = _rope(q.reshape(B, T, H_SWA, HEAD_DIM), pos)
    k = _rope(k.reshape(B, T, KV_SWA, HEAD_DIM), pos)
    v = v.reshape(B, T, KV_SWA, HEAD_DIM)
    kv_new = jnp.stack([k, v], axis=2)
    if past is None:
        nC = T // CHUNK
        def band(a):
            a = jnp.pad(a.reshape(B, nC, CHUNK, KV_SWA, HEAD_DIM), ((0, 0), (W_CHUNKS, 0), (0, 0), (0, 0), (0, 0)))
            return jnp.concatenate([a[:, j:j + nC] for j in range(W_CHUNKS + 1)], axis=2)
        qpos = pos.reshape(nC, CHUNK)
        kpos = (jnp.arange(nC, dtype=jnp.int32)[:, None] - W_CHUNKS) * CHUNK + jnp.arange((W_CHUNKS + 1) * CHUNK, dtype=jnp.int32)[None, :]
        o = _sink_attend(q.reshape(B, nC, CHUNK, KV_SWA, G, HEAD_DIM), band(k), band(v), _band_valid(qpos, kpos), sinks)
        new_state = kv_new[:, T - min(WINDOW, T):]
    else:
        W = past.shape[1]
        k_all = jnp.concatenate([past[:, :, 0], k], axis=1)[:, None]
        v_all = jnp.concatenate([past[:, :, 1], v], axis=1)[:, None]
        kpos = jnp.concatenate([pos[0] - W + jnp.arange(W, dtype=jnp.int32), pos])
        o = _sink_attend(q.reshape(B, 1, T, KV_SWA, G, HEAD_DIM), k_all, v_all, _band_valid(pos, kpos)[None], sinks)
        new_state = jnp.concatenate([past, kv_new], axis=1)[:, T:]
    return o.reshape(B, T, H_SWA * HEAD_DIM) @ w_out, new_state


def _clamped_swiglu(gu):
    g, u = jnp.split(gu, 2, axis=-1)
    g = jnp.minimum(g, SWIGLU_LIMIT)
    u = jnp.clip(u, -SWIGLU_LIMIT, SWIGLU_LIMIT)
    return g * jax.nn.sigmoid(SWIGLU_ALPHA * g) * (u + 1)


def _moe(h, w_router, b_router, w_gu, b_gu, w_dn, b_dn):
    B, T, D = h.shape
    N = B * T
    x = h.reshape(N, D)
    logits = (x @ w_router + b_router).astype(jnp.float32)
    top_logit, top_e = lax.top_k(logits, TOP_K)
    gate = jax.nn.softmax(top_logit, axis=-1)
    e_flat = top_e.reshape(-1)
    order = jnp.argsort(e_flat)
    e_sorted = e_flat[order]
    tok_sorted = order // TOP_K
    g_sorted = gate.reshape(-1)[order]
    counts = jnp.bincount(e_flat, length=N_EXPERTS)
    start = jnp.cumsum(counts) - counts
    padded = (counts + EXPERT_BLOCK - 1) // EXPERT_BLOCK * EXPERT_BLOCK
    pend = jnp.cumsum(padded)
    dest = pend[e_sorted] - padded[e_sorted] + jnp.arange(N * TOP_K, dtype=jnp.int32) - start[e_sorted]
    n_blocks = -(-N * TOP_K // EXPERT_BLOCK) + N_EXPERTS
    xs = jnp.zeros((n_blocks * EXPERT_BLOCK, D), x.dtype).at[dest].set(x[tok_sorted])
    blk_e = jnp.minimum(jnp.searchsorted(pend, jnp.arange(n_blocks, dtype=jnp.int32) * EXPERT_BLOCK, side='right'), N_EXPERTS - 1)
    def expert_block(args):
        xb, e = args
        return _clamped_swiglu(xb @ w_gu[e] + b_gu[e]) @ w_dn[e] + b_dn[e]
    ys = lax.map(expert_block, (xs.reshape(n_blocks, EXPERT_BLOCK, D), blk_e)).reshape(-1, D)
    y = jnp.zeros((N, D), x.dtype).at[tok_sorted].add(ys[dest] * g_sorted[:, None].astype(x.dtype))
    return y.reshape(B, T, D)


def _trunk(x, c, pos, past, w_ada, b_ada, g_mix, g_ffn, w_in_ab, b_forget, w_out_ab,
           w_in_c, sinks_c, w_out_c, w_router, b_router, w_gu, b_gu, w_dn, b_dn, g_final):
    st_ab, st_c = [], []
    for l in range(DEPTH):
        mod = jax.nn.silu(c) @ w_ada[l] + b_ada[l]
        sh1, sc1, g1, sh2, sc2, g2 = jnp.split(mod[:, None, :], 6, axis=-1)
        h = _rms(x, g_mix[l]) * (1 + sc1) + sh1
        j = l // 2
        if l % 2 == 0:
            p = None if past is None else (past[0][j], past[1][j], past[2][j], past[3][j])
            o, st = _mixer_ab(h, pos, w_in_ab[j], b_forget[j], w_out_ab[j], p)
            st_ab.append(st)
        else:
            p = None if past is None else past[4][j]
            o, st = _mixer_c(h, pos, w_in_c[j], sinks_c[j], w_out_c[j], p)
            st_c.append(st)
        x = x + g1 * o
        h = _rms(x, g_ffn[l]) * (1 + sc2) + sh2
        x = x + g2 * _moe(h, w_router[l], b_router[l], w_gu[l], b_gu[l], w_dn[l], b_dn[l])
    y = _rms(x, g_final)
    states = (jnp.stack([s[0] for s in st_ab]), jnp.stack([s[1] for s in st_ab]),
              jnp.stack([s[2] for s in st_ab]), jnp.stack([s[3] for s in st_ab]), jnp.stack(st_c))
    return y, states


def setup_inputs(seed: int = 0) -> dict:
    key = jax.random.key(seed)
    ks = jax.random.split(key, 26)
    def nrm(k, shape, s=1.0):
        return s * jax.random.normal(k, shape, jnp.float32)
    D = D_MODEL
    W = min(WINDOW, PAST_LEN)
    return {
        'x_prompt': nrm(ks[0], (BATCH, SEQ, D)),
        'x_sample': nrm(ks[1], (DEC_BATCH, DEC_SEQ, D)),
        'c_prompt': nrm(ks[2], (BATCH, D)),
        'c_sample': nrm(ks[3], (DEC_BATCH, D)),
        'cache_fox_kv': nrm(ks[4], (N_AB, DEC_BATCH, PAST_LEN, 2, H_FOX, HEAD_DIM)),
        'cache_fox_logf': jax.nn.log_sigmoid(4.0 + nrm(ks[5], (N_AB, DEC_BATCH, PAST_LEN, H_FOX))),
        'cache_dsa_kv': nrm(ks[6], (N_AB, DEC_BATCH, PAST_LEN, 2, KV_DSA, HEAD_DIM)),
        'cache_dsa_kidx': nrm(ks[7], (N_AB, DEC_BATCH, PAST_LEN, D_IDX)),
        'cache_swa_kv': nrm(ks[8], (N_C, DEC_BATCH, W, 2, KV_SWA, HEAD_DIM)),
        'w_ada': nrm(ks[9], (DEPTH, D, 6 * D), 0.5 * D ** -0.5),
        'b_ada': nrm(ks[10], (DEPTH, 6 * D), 0.02),
        'g_mix': 1.0 + nrm(ks[11], (DEPTH, D), 0.02),
        'g_ffn': 1.0 + nrm(ks[12], (DEPTH, D), 0.02),
        'w_in_ab': nrm(ks[13], (N_AB, D, D_IN_AB), D ** -0.5),
        'b_forget': 4.0 + nrm(ks[14], (N_AB, H_FOX), 0.5),
        'w_out_ab': nrm(ks[15], (N_AB, D_MIX_AB, D), D_MIX_AB ** -0.5),
        'w_in_c': nrm(ks[16], (N_C, D, D_IN_C), D ** -0.5),
        'sinks_c': nrm(ks[17], (N_C, H_SWA), 0.5),
        'w_out_c': nrm(ks[18], (N_C, H_SWA * HEAD_DIM, D), (H_SWA * HEAD_DIM) ** -0.5),
        'w_router': nrm(ks[19], (DEPTH, D, N_EXPERTS), D ** -0.5),
        'b_router': nrm(ks[20], (DEPTH, N_EXPERTS), 0.01),
        'w_gu': nrm(ks[21], (DEPTH, N_EXPERTS, D, 2 * D_FF), D ** -0.5),
        'b_gu': nrm(ks[22], (DEPTH, N_EXPERTS, 2 * D_FF), 0.01),
        'w_dn': nrm(ks[23], (DEPTH, N_EXPERTS, D_FF, D), D_FF ** -0.5),
        'b_dn': nrm(ks[24], (DEPTH, N_EXPERTS, D), 0.01),
        'g_final': 1.0 + nrm(ks[25], (D,), 0.02),
    }


def reference(x_prompt, x_sample, c_prompt, c_sample, cache_fox_kv, cache_fox_logf, cache_dsa_kv,
              cache_dsa_kidx, cache_swa_kv, w_ada, b_ada, g_mix, g_ffn, w_in_ab, b_forget, w_out_ab,
              w_in_c, sinks_c, w_out_c, w_router, b_router, w_gu, b_gu, w_dn, b_dn, g_final):
    weights = (w_ada, b_ada, g_mix, g_ffn, w_in_ab, b_forget, w_out_ab, w_in_c, sinks_c, w_out_c,
               w_router, b_router, w_gu, b_gu, w_dn, b_dn, g_final)
    pos_p = jnp.arange(x_prompt.shape[1], dtype=jnp.int32)
    pos_s = PAST_LEN + jnp.arange(x_sample.shape[1], dtype=jnp.int32)
    y_prompt, (fkv_p, flog_p, dkv_p, dki_p, skv_p) = _trunk(x_prompt, c_prompt, pos_p, None, *weights)
    past = (cache_fox_kv, cache_fox_logf, cache_dsa_kv, cache_dsa_kidx, cache_swa_kv)
    y_sample, (fkv_s, flog_s, dkv_s, dki_s, skv_s) = _trunk(x_sample, c_sample, pos_s, past, *weights)
    return (y_prompt, y_sample, fkv_p, fkv_s, flog_p, flog_s, dkv_p, dkv_s, dki_p, dki_s, skv_p, skv_s)
```

```python
import functools

import numpy as np
import jax
import jax.numpy as jnp
from jax import lax
from jax.experimental import pallas as pl
from jax.experimental.pallas import tpu as pltpu

CHUNK = 64
HEAD_DIM = 64
ROPE_THETA = 10000.0
EPS = 1e-5
NEG = -1e30
TOPK_MAX = 256
TOP_K = 4
SWIGLU_ALPHA = 1.702
SWIGLU_LIMIT = 7.0

LANES = 128
F32 = jnp.float32
MXU_DTYPE = jnp.bfloat16
VMEM_LIMIT = 56 * 1024 * 1024

INT_MIN = -2 ** 31
_negbits = int(np.array(NEG, np.float32).view(np.int32))
KEY_NEG = _negbits ^ 0x7FFFFFFF

_NT = (((1,), (1,)), ((), ()))


def _params(*sem):
    return pltpu.CompilerParams(dimension_semantics=sem, vmem_limit_bytes=VMEM_LIMIT)


def _tile(n, pref, mult=LANES):
    if n <= pref:
        return n
    t = pref - pref % mult
    while t >= mult:
        if n % t == 0:
            return t
        t -= mult
    return n


def _sigmoid(x):
    return 1.0 / (1.0 + jnp.exp(-x))


def _ada_body(c_ref, w_ref, b_ref, o_ref):
    c = c_ref[...]
    a = (c * _sigmoid(c)).astype(MXU_DTYPE)
    o_ref[0] = jnp.dot(a, w_ref[0].astype(MXU_DTYPE), preferred_element_type=F32) + b_ref[0]


def _ada(c_all, w_ada, b_ada):
    L, D, N = w_ada.shape
    R = c_all.shape[0]
    tn = _tile(N, 1024)
    return pl.pallas_call(
        _ada_body,
        grid=(L, N // tn),
        in_specs=[pl.BlockSpec((R, D), lambda l, j: (0, 0)),
                  pl.BlockSpec((1, D, tn), lambda l, j: (l, 0, j)),
                  pl.BlockSpec((1, 1, tn), lambda l, j: (l, 0, j))],
        out_specs=pl.BlockSpec((1, R, tn), lambda l, j: (l, 0, j)),
        out_shape=jax.ShapeDtypeStruct((L, R, N), F32),
        compiler_params=_params("arbitrary", "arbitrary"),
        name="ada_mod",
    )(c_all, w_ada, b_ada.reshape(L, 1, N))


def _norm_mod_body(x_ref, g_ref, sh_ref, sc_ref, o_ref):
    x = x_ref[...]
    ms = jnp.mean(x * x, axis=-1, keepdims=True)
    xn = x * lax.rsqrt(ms + EPS) * g_ref[...]
    o_ref[...] = (xn * (1.0 + sc_ref[...]) + sh_ref[...]).astype(o_ref.dtype)


def _norm_body(x_ref, g_ref, o_ref):
    x = x_ref[...]
    ms = jnp.mean(x * x, axis=-1, keepdims=True)
    o_ref[...] = (x * lax.rsqrt(ms + EPS) * g_ref[...]).astype(o_ref.dtype)


def _norm_mod(x3, g, mod_exp, k_shift, k_scale):
    NB, C, D = x3.shape
    bb = _tile(NB, 8, 1)
    return pl.pallas_call(
        _norm_mod_body,
        grid=(NB // bb,),
        in_specs=[pl.BlockSpec((bb, C, D), lambda i: (i, 0, 0)),
                  pl.BlockSpec((1, 1, D), lambda i: (0, 0, 0)),
                  pl.BlockSpec((bb, 1, D), lambda i: (i, 0, k_shift)),
                  pl.BlockSpec((bb, 1, D), lambda i: (i, 0, k_scale))],
        out_specs=pl.BlockSpec((bb, C, D), lambda i: (i, 0, 0)),
        out_shape=jax.ShapeDtypeStruct((NB, C, D), MXU_DTYPE),
        compiler_params=_params("arbitrary"),
        name="norm_mod",
    )(x3, g.reshape(1, 1, D), mod_exp, mod_exp)


def _final_norm(x3, g):
    NB, C, D = x3.shape
    bb = _tile(NB, 8, 1)
    return pl.pallas_call(
        _norm_body,
        grid=(NB // bb,),
        in_specs=[pl.BlockSpec((bb, C, D), lambda i: (i, 0, 0)),
                  pl.BlockSpec((1, 1, D), lambda i: (0, 0, 0))],
        out_specs=pl.BlockSpec((bb, C, D), lambda i: (i, 0, 0)),
        out_shape=jax.ShapeDtypeStruct((NB, C, D), F32),
        compiler_params=_params("arbitrary"),
        name="final_norm",
    )(x3, g.reshape(1, 1, D))


def _rope_tile(acc, cos, sin):
    tn = acc.shape[1]
    reps = tn // LANES
    cosf = jnp.concatenate([cos] * reps, axis=1) if reps > 1 else cos
    sinf = jnp.concatenate([sin] * reps, axis=1) if reps > 1 else sin
    lane = lax.broadcasted_iota(jnp.int32, acc.shape, 1)
    first_half = (lane & (HEAD_DIM - 1)) < HEAD_DIM // 2
    rot = jnp.where(first_half, pltpu.roll(acc, tn - HEAD_DIM // 2, 1), pltpu.roll(acc, HEAD_DIM // 2, 1))
    return acc * cosf + rot * sinf


def _mm_body(*refs, scale, rope_tiles, n_col_tiles, has_rope, has_bias):
    x_ref, w_ref = refs[:2]
    idx = 2
    if has_rope:
        cos_ref, sin_ref = refs[idx:idx + 2]
        idx += 2
    if has_bias:
        b_ref = refs[idx]
        idx += 1
    out_refs = refs[idx:]
    acc = jnp.dot(x_ref[...], w_ref[...].astype(MXU_DTYPE), preferred_element_type=F32)
    if has_bias:
        acc = acc + b_ref[...]

    def emit(val):
        if scale != 1.0:
            val = val * scale
        for o in out_refs:
            o[...] = val.astype(o.dtype)

    if has_rope and rope_tiles > 0:
        if rope_tiles >= n_col_tiles:
            emit(_rope_tile(acc, cos_ref[...], sin_ref[...]))
        else:
            j = pl.program_id(1)

            @pl.when(j < rope_tiles)
            def _():
                emit(_rope_tile(acc, cos_ref[...], sin_ref[...]))

            @pl.when(j >= rope_tiles)
            def _():
                emit(acc)
    else:
        emit(acc)


def _mm(x, w, out_dtypes, *, tm=1024, tn=512, scale=1.0, rope=None, rope_tiles=0, bias=None):
    M, K = x.shape
    N = w.shape[1]
    tm = _tile(M, tm, 64)
    tn = _tile(N, tn)
    n_col = N // tn
    in_specs = [pl.BlockSpec((tm, K), lambda i, j: (i, 0)),
                pl.BlockSpec((K, tn), lambda i, j: (0, j))]
    args = [x, w]
    if rope is not None:
        in_specs += [pl.BlockSpec((tm, LANES), lambda i, j: (i, 0))] * 2
        args += list(rope)
    if bias is not None:
        in_specs.append(pl.BlockSpec((1, tn), lambda i, j: (0, j)))
        args.append(bias)
    outs = pl.pallas_call(
        functools.partial(_mm_body, scale=scale, rope_tiles=rope_tiles if rope is not None else 0,
                          n_col_tiles=n_col, has_rope=rope is not None, has_bias=bias is not None),
        grid=(M // tm, n_col),
        in_specs=in_specs,
        out_specs=[pl.BlockSpec((tm, tn), lambda i, j: (i, j)) for _ in out_dtypes],
        out_shape=[jax.ShapeDtypeStruct((M, N), dt) for dt in out_dtypes],
        compiler_params=_params("arbitrary", "arbitrary"),
        name="proj",
    )(*args)
    return outs


def _small_body(x_ref, w_ref, cos_ref, sin_ref, bf_ref, o_ref, *, d_idx, h_fox, wi_scale):
    acc = jnp.dot(x_ref[...], w_ref[...].astype(MXU_DTYPE), preferred_element_type=F32)
    roped = _rope_tile(acc, cos_ref[...], sin_ref[...])
    z = acc + bf_ref[...]
    logf = jnp.minimum(z, 0.0) - jnp.log1p(jnp.exp(-jnp.abs(z)))
    lane = lax.broadcasted_iota(jnp.int32, acc.shape, 1)
    o_ref[...] = jnp.where(lane < d_idx, roped, jnp.where(lane < d_idx + h_fox, logf, acc * wi_scale))


def _mm_small(x, w, rope, bf_row, *, d_idx, h_fox, wi_scale, tm=1024):
    M, K = x.shape
    tm = _tile(M, tm, 64)
    return pl.pallas_call(
        functools.partial(_small_body, d_idx=d_idx, h_fox=h_fox, wi_scale=wi_scale),
        grid=(M // tm,),
        in_specs=[pl.BlockSpec((tm, K), lambda i: (i, 0)),
                  pl.BlockSpec((K, LANES), lambda i: (0, 0)),
                  pl.BlockSpec((tm, LANES), lambda i: (i, 0)),
                  pl.BlockSpec((tm, LANES), lambda i: (i, 0)),
                  pl.BlockSpec((1, LANES), lambda i: (0, 0))],
        out_specs=pl.BlockSpec((tm, LANES), lambda i: (i, 0)),
        out_shape=jax.ShapeDtypeStruct((M, LANES), F32),
        compiler_params=_params("arbitrary"),
        name="proj_small",
    )(x, w, rope[0], rope[1], bf_row)


def _mm_res_body(*refs, n_pairs):
    xs = refs[:n_pairs]
    ws = refs[n_pairs:2 * n_pairs]
    res_ref, gate_ref, o_ref = refs[2 * n_pairs:]
    acc = None
    for x_ref, w_ref in zip(xs, ws):
        part = jnp.dot(x_ref[...], w_ref[0].astype(MXU_DTYPE), preferred_element_type=F32)
        acc = part if acc is None else acc + part
    tm, tn = acc.shape
    o_ref[...] = res_ref[...] + gate_ref[...] * acc.reshape(tm // CHUNK, CHUNK, tn)


def _mm_res(xs, w_all, layer, res3, mod_exp, k_gate, *, tm=1024, tn=512):
    NB, C, N = res3.shape
    M = NB * C
    tm = _tile(M, tm, C)
    tn = _tile(N, tn)
    n_col = N // tn
    kp = xs[0].shape[1]
    assert all(x.shape == (M, kp) for x in xs) and w_all.shape[1] == kp * len(xs)
    in_specs = [pl.BlockSpec((tm, kp), lambda i, j: (i, 0)) for _ in xs]
    in_specs += [pl.BlockSpec((1, kp, tn), functools.partial(lambda i, j, p: (layer, p, j), p=p))
                 for p in range(len(xs))]
    in_specs += [pl.BlockSpec((tm // C, C, tn), lambda i, j: (i, 0, j)),
                 pl.BlockSpec((tm // C, 1, tn), lambda i, j: (i, 0, k_gate * n_col + j))]
    return pl.pallas_call(
        functools.partial(_mm_res_body, n_pairs=len(xs)),
        grid=(M // tm, n_col),
        in_specs=in_specs,
        out_specs=pl.BlockSpec((tm // C, C, tn), lambda i, j: (i, 0, j)),
        out_shape=jax.ShapeDtypeStruct((NB, C, N), F32),
        compiler_params=_params("arbitrary", "arbitrary"),
        name="out_proj_res",
    )(*xs, *([w_all] * len(xs)), res3, mod_exp)


def _fox_body(q_ref, k_ref, v_ref, cq_ref, ck_ref, o_ref, *, tq, tk, q_off, nq):
    q0 = q_off if nq == 1 else q_off + pl.program_id(2) * tq
    n_full = q0 // tk
    q = q_ref[0]
    lo = lax.broadcasted_iota(jnp.int32, (tq, LANES), 1) < HEAD_DIM
    zero = jnp.zeros_like(q)
    qh = (jnp.where(lo, q, zero), jnp.where(lo, zero, q))
    cq = cq_ref[0, 0]
    cqh = (cq[:, 0:1], cq[:, 1:2])

    def chunk(c, size, carry, masked):
        k0 = c * tk
        if not isinstance(k0, int):
            k0 = pl.multiple_of(k0, tk)
        kc = k_ref[0, pl.ds(k0, size), :]
        vc = v_ref[0, pl.ds(k0, size), :]
        out = []
        for h in range(2):
            m, l, acc = carry[3 * h:3 * h + 3]
            s = lax.dot_general(qh[h], kc, _NT, preferred_element_type=F32)
            ck = ck_ref[0, 0, h, pl.ds(c, 1), :][:, :size]
            s = s + (cqh[h] - ck)
            if masked:
                row = lax.broadcasted_iota(jnp.int32, s.shape, 0)
                col = lax.broadcasted_iota(jnp.int32, s.shape, 1)
                s = jnp.where(col <= row, s, NEG)
            m_new = jnp.maximum(m, jnp.max(s, axis=1, keepdims=True))
            alpha = jnp.exp(m - m_new)
            p = jnp.exp(s - m_new)
            l = alpha * l + jnp.sum(p, axis=1, keepdims=True)
            acc = alpha * acc + jnp.dot(p.astype(MXU_DTYPE), vc, preferred_element_type=F32)
            out += [m_new, l, acc]
        return tuple(out)

    init = (jnp.full((tq, 1), NEG, F32), jnp.zeros((tq, 1), F32), jnp.zeros((tq, LANES), F32)) * 2
    carry = lax.fori_loop(0, n_full, lambda c, cr: chunk(c, tk, cr, False), init)
    carry = chunk(n_full, tq, carry, True)
    o = jnp.where(lo, carry[2] / carry[1], carry[5] / carry[4])
    o_ref[0] = o.astype(o_ref.dtype)


def _fox_attention(q, kv, cum_q, cum_k, *, tq, tk):
    B, T, HD = q.shape
    S = kv.shape[1]
    n_pairs = HD // LANES
    q_off = S - T
    nq = T // tq
    assert T % tq == 0 and q_off % tk == 0 and (tq == tk or (nq == 1 and tq <= tk))
    s_pad = -(-S // tk) * tk
    cq = cum_q.reshape(B, T, n_pairs, 2).transpose(0, 2, 1, 3)
    ck = jnp.pad(cum_k, ((0, 0), (0, s_pad - S), (0, 0)))
    ck = ck.reshape(B, s_pad // tk, tk, n_pairs, 2).transpose(0, 3, 4, 1, 2)
    return pl.pallas_call(
        functools.partial(_fox_body, tq=tq, tk=tk, q_off=q_off, nq=nq),
        grid=(B, n_pairs, nq),
        in_specs=[pl.BlockSpec((1, tq, LANES), lambda b, p, i: (b, i, p)),
                  pl.BlockSpec((1, S, LANES), lambda b, p, i: (b, 0, p)),
                  pl.BlockSpec((1, S, LANES), lambda b, p, i: (b, 0, n_pairs + p)),
                  pl.BlockSpec((1, 1, tq, 2), lambda b, p, i: (b, p, i, 0)),
                  pl.BlockSpec((1, 1, 2, s_pad // tk, tk), lambda b, p, i: (b, p, 0, 0, 0))],
        out_specs=pl.BlockSpec((1, tq, LANES), lambda b, p, i: (b, i, p)),
        out_shape=jax.ShapeDtypeStruct((B, T, HD), MXU_DTYPE),
        compiler_params=_params("arbitrary", "arbitrary", "arbitrary"),
        name="fox_attention",
    )(q, kv, kv, cq, ck)


def _swap_halves(x):
    return jnp.concatenate([x[:, HEAD_DIM:], x[:, :HEAD_DIM]], axis=1)


def _dsa_body(q_ref, k_ref, v_ref, qi_ref, kid_ref, wi_ref, o_ref,
              keys_scr, bias_scr, m_scr, l_scr, acc_scr, j_scr, *,
              tq, tk, q_off, s_real, topk, n_heads, n_kv, n_idx, nq, n_chunks):
    if nq == 1:
        q0 = q_off
        n_ch = min((q0 + tq + tk - 1) // tk, n_chunks)
        unscanned = float(s_real - min(n_ch * tk, s_real))
    else:
        q0 = q_off + pl.program_id(1) * tq
        n_ch = jnp.minimum((q0 + tq + tk - 1) // tk, n_chunks)
        unscanned = (s_real - jnp.minimum(n_ch * tk, s_real)).astype(F32)
    row = lax.broadcasted_iota(jnp.int32, (tq, tk), 0)
    col = lax.broadcasted_iota(jnp.int32, (tq, tk), 1)
    qchunk = (q0 + row) // CHUNK
    lo = lax.broadcasted_iota(jnp.int32, (tq, LANES), 1) < HEAD_DIM

    def admissible(kpos):
        return ((kpos // CHUNK) <= qchunk) & (kpos < s_real)

    qi = qi_ref[0]
    qim = []
    for p in range(n_idx // 2):
        blk = qi[:, p * LANES:(p + 1) * LANES]
        zero = jnp.zeros_like(blk)
        qim += [jnp.where(lo, blk, zero), jnp.where(lo, zero, blk)]
    wi = wi_ref[0]
    wcols = [wi[:, h:h + 1] for h in range(n_idx)]

    def score_chunk(c, _):
        kc = kid_ref[0, pl.ds(pl.multiple_of(c * tk, tk), tk), :]
        score = jnp.zeros((tq, tk), F32)
        for h in range(n_idx):
            s = lax.dot_general(qim[h], kc, _NT, preferred_element_type=F32)
            score = score + wcols[h] * jnp.maximum(s, 0.0)
        kpos = c * tk + col
        sm = jnp.where(admissible(kpos), score, NEG)
        bits = pltpu.bitcast(sm, jnp.int32)
        key = jnp.where(bits < 0, bits ^ 0x7FFFFFFF, bits)
        key = jnp.where(bits == INT_MIN, 0, key)
        keys_scr[c] = jnp.where(kpos < s_real, key, INT_MIN)
        return 0

    lax.fori_loop(0, n_ch, score_chunk, 0)

    def count(pred):
        def body(c, acc):
            x = jnp.where(pred(keys_scr[c], c), 1.0, 0.0)
            for j in range(tk // LANES):
                acc = acc + x[:, j * LANES:(j + 1) * LANES]
            return acc
        acc = lax.fori_loop(0, n_ch, body, jnp.zeros((tq, LANES), F32))
        return jnp.sum(acc, axis=1, keepdims=True)

    kf = float(topk)

    def bit_step(it, carry):
        t, cge = carry
        cand = t + jnp.left_shift(jnp.int32(1), 31 - it)
        cnt = count(lambda kc, c: kc >= cand) + jnp.where(cand <= KEY_NEG, unscanned, 0.0)
        ok = cnt >= kf
        return jnp.where(ok, cand, t), jnp.where(ok, cnt, cge)

    thr, cge = lax.fori_loop(0, 32, bit_step,
                             (jnp.full((tq, 1), INT_MIN, jnp.int32), jnp.full((tq, 1), 3e38, F32)))
    cgt = count(lambda kc, c: kc > thr) + jnp.where(thr < KEY_NEG, unscanned, 0.0)
    need = kf - cgt

    n_bits = int(n_chunks * tk).bit_length()
    j_scr[...] = jnp.full((tq, 1), 2 ** n_bits, jnp.int32)

    @pl.when(jnp.max(cge) > kf)
    def _():
        def j_step(it, jcur):
            cand = jcur + jnp.left_shift(jnp.int32(1), n_bits - 1 - it)
            f = count(lambda kc, c: (kc == thr) & ((c * tk + col) < cand))
            return jnp.where(f < need, cand, jcur)
        j_scr[...] = lax.fori_loop(0, n_bits, j_step, jnp.zeros((tq, 1), jnp.int32))

    jmax = j_scr[...]

    def bias_chunk(c, _):
        kc = keys_scr[c]
        kpos = c * tk + col
        sel = (kc > thr) | ((kc == thr) & (kpos <= jmax))
        bias_scr[c] = jnp.where(sel & admissible(kpos), 0.0, NEG)
        return 0

    lax.fori_loop(0, n_ch, bias_chunk, 0)

    q = q_ref[0]
    G = n_heads // n_kv
    outs = []
    for g in range(n_kv):
        r, kh = g // 2, g % 2
        qms = []
        for jh in range(G):
            h = g * G + jh
            blk = q[:, (h // 2) * LANES:(h // 2 + 1) * LANES]
            if h % 2 != kh:
                blk = _swap_halves(blk)
            zero = jnp.zeros_like(blk)
            qms.append(jnp.where(lo, blk, zero) if kh == 0 else jnp.where(lo, zero, blk))
        m_scr[...] = jnp.full(m_scr.shape, NEG, F32)
        l_scr[...] = jnp.zeros(l_scr.shape, F32)
        acc_scr[...] = jnp.zeros(acc_scr.shape, F32)

        def attend(c, _):
            k0 = pl.multiple_of(c * tk, tk)
            kc = k_ref[0, pl.ds(k0, tk), r * LANES:(r + 1) * LANES]
            vc = v_ref[0, pl.ds(k0, tk), r * LANES:(r + 1) * LANES]
            b = bias_scr[c]
            for jh in range(G):
                s = lax.dot_general(qms[jh], kc, _NT, preferred_element_type=F32) + b
                m = m_scr[jh]
                m_new = jnp.maximum(m, jnp.max(s, axis=1, keepdims=True))
                alpha = jnp.exp(m - m_new)
                p = jnp.exp(s - m_new)
                l_scr[jh] = alpha * l_scr[jh] + jnp.sum(p, axis=1, keepdims=True)
                acc_scr[jh] = alpha * acc_scr[jh] + jnp.dot(p.astype(MXU_DTYPE), vc, preferred_element_type=F32)
                m_scr[jh] = m_new
            return 0

        lax.fori_loop(0, n_ch, attend, 0)
        for jh in range(G):
            outs.append((acc_scr[jh] / l_scr[jh], kh))

    for pr in range(n_heads // 2):
        (o0, h0), (o1, h1) = outs[2 * pr], outs[2 * pr + 1]
        a = o0 if h0 == 0 else pltpu.roll(o0, HEAD_DIM, 1)
        b = o1 if h1 == 1 else pltpu.roll(o1, HEAD_DIM, 1)
        o_ref[0, :, pr * LANES:(pr + 1) * LANES] = jnp.where(lo, a, b).astype(o_ref.dtype)


def _dsa_attention(q, k, v, qi, kid, wi, *, s_real, topk, tq, tk):
    B, T, HD = q.shape
    s_pad = k.shape[1]
    n_heads, n_kv, n_idx = HD // HEAD_DIM, k.shape[2] // HEAD_DIM, qi.shape[2] // HEAD_DIM
    assert n_kv % 2 == 0 and n_heads % n_kv == 0 and n_idx % 2 == 0 and s_pad % tk == 0 and T % tq == 0
    G = n_heads // n_kv
    nq = T // tq
    n_chunks = s_pad // tk
    body = functools.partial(_dsa_body, tq=tq, tk=tk, q_off=s_real - T, s_real=s_real, topk=topk,
                             n_heads=n_heads, n_kv=n_kv, n_idx=n_idx, nq=nq, n_chunks=n_chunks)
    return pl.pallas_call(
        body,
        grid=(B, nq),
        in_specs=[pl.BlockSpec((1, tq, HD), lambda b, i: (b, i, 0)),
                  pl.BlockSpec((1, s_pad, k.shape[2]), lambda b, i: (b, 0, 0)),
                  pl.BlockSpec((1, s_pad, v.shape[2]), lambda b, i: (b, 0, 0)),
                  pl.BlockSpec((1, tq, qi.shape[2]), lambda b, i: (b, i, 0)),
                  pl.BlockSpec((1, s_pad, LANES), lambda b, i: (b, 0, 0)),
                  pl.BlockSpec((1, tq, n_idx), lambda b, i: (b, i, 0))],
        out_specs=pl.BlockSpec((1, tq, HD), lambda b, i: (b, i, 0)),
        out_shape=jax.ShapeDtypeStruct((B, T, HD), MXU_DTYPE),
        scratch_shapes=[pltpu.VMEM((n_chunks, tq, tk), jnp.int32),
                        pltpu.VMEM((n_chunks, tq, tk), F32),
                        pltpu.VMEM((G, tq, 1), F32),
                        pltpu.VMEM((G, tq, 1), F32),
                        pltpu.VMEM((G, tq, LANES), F32),
                        pltpu.VMEM((tq, 1), jnp.int32)],
        compiler_params=_params("arbitrary", "arbitrary"),
        name="dsa_attention",
    )(q, k, v, qi, kid, wi)


def _swa_body(sinks_ref, q_ref, k0_ref, k1_ref, k2_ref, v0_ref, v1_ref, v2_ref, o_ref, *,
              first_valid, n_heads, n_kv):
    c = pl.program_id(1)
    q = q_ref[0]
    k = jnp.concatenate([k0_ref[0], k1_ref[0], k2_ref[0]], axis=0)
    v = jnp.concatenate([v0_ref[0], v1_ref[0], v2_ref[0]], axis=0)
    n_keys = k.shape[0]
    valid = (c * CHUNK + lax.broadcasted_iota(jnp.int32, (CHUNK, n_keys), 1)) >= first_valid
    G = n_heads // n_kv
    for pr in range(n_heads // 2):
        halves = []
        for h in (2 * pr, 2 * pr + 1):
            g = h // G
            qh = q[:, h * HEAD_DIM:(h + 1) * HEAD_DIM]
            kg = k[:, g * HEAD_DIM:(g + 1) * HEAD_DIM]
            vg = v[:, g * HEAD_DIM:(g + 1) * HEAD_DIM]
            s = lax.dot_general(qh, kg, _NT, preferred_element_type=F32)
            s = jnp.where(valid, s, NEG)
            sink = sinks_ref[h]
            m = jnp.maximum(jnp.max(s, axis=1, keepdims=True), sink)
            e = jnp.exp(s - m)
            p = e / (jnp.sum(e, axis=1, keepdims=True) + jnp.exp(sink - m))
            halves.append(jnp.dot(p.astype(MXU_DTYPE), vg, preferred_element_type=F32))
        o_ref[0, :, pr * LANES:(pr + 1) * LANES] = jnp.concatenate(halves, axis=1).astype(o_ref.dtype)


def _swa_attention(q, kpad, vpad, sinks, *, first_valid):
    B, T, HD = q.shape
    KD = kpad.shape[2]
    n_heads, n_kv = HD // HEAD_DIM, KD // HEAD_DIM
    kv_specs = [pl.BlockSpec((1, CHUNK, KD), functools.partial(lambda b, c, s, o: (b, c + o, 0), o=o))
                for o in range(3)]
    return pl.pallas_call(
        functools.partial(_swa_body, first_valid=first_valid, n_heads=n_heads, n_kv=n_kv),
        grid_spec=pltpu.PrefetchScalarGridSpec(
            num_scalar_prefetch=1,
            grid=(B, T // CHUNK),
            in_specs=[pl.BlockSpec((1, CHUNK, HD), lambda b, c, s: (b, c, 0))] + kv_specs + kv_specs,
            out_specs=pl.BlockSpec((1, CHUNK, HD), lambda b, c, s: (b, c, 0))),
        out_shape=jax.ShapeDtypeStruct((B, T, HD), MXU_DTYPE),
        compiler_params=_params("arbitrary", "arbitrary"),
        name="swa_attention",
    )(sinks.astype(F32), q, kpad, kpad, kpad, vpad, vpad, vpad)


def _gmm1_body(be_ref, nu_ref, x_ref, wg_ref, wu_ref, bg_ref, bu_ref, o_ref, wg_s, wu_s):
    m = pl.program_id(1)
    changed = (m == 0) | (be_ref[m] != be_ref[jnp.maximum(m - 1, 0)])

    @pl.when(changed)
    def _():
        wg_s[...] = wg_ref[0, 0].astype(MXU_DTYPE)
        wu_s[...] = wu_ref[0, 0].astype(MXU_DTYPE)

    @pl.when(m < nu_ref[0])
    def _():
        x = x_ref[...]
        g = jnp.dot(x, wg_s[...], preferred_element_type=F32) + bg_ref[0, 0]
        u = jnp.dot(x, wu_s[...], preferred_element_type=F32) + bu_ref[0, 0]
        g = jnp.minimum(g, SWIGLU_LIMIT)
        u = jnp.clip(u, -SWIGLU_LIMIT, SWIGLU_LIMIT)
        o_ref[...] = (g * _sigmoid(SWIGLU_ALPHA * g) * (u + 1.0)).astype(o_ref.dtype)

    @pl.when(m >= nu_ref[0])
    def _():
        o_ref[...] = jnp.zeros(o_ref.shape, o_ref.dtype)


def _gmm2_body(be_ref, nu_ref, a_ref, w_ref, b_ref, o_ref, w_s):
    m = pl.program_id(1)
    changed = (m == 0) | (be_ref[m] != be_ref[jnp.maximum(m - 1, 0)])

    @pl.when(changed)
    def _():
        w_s[...] = w_ref[0, 0].astype(MXU_DTYPE)

    @pl.when(m < nu_ref[0])
    def _():
        o_ref[...] = jnp.dot(a_ref[...], w_s[...], preferred_element_type=F32) + b_ref[0, 0]

    @pl.when(m >= nu_ref[0])
    def _():
        o_ref[...] = jnp.zeros(o_ref.shape, o_ref.dtype)


def _moe_experts(xs, blk_e, n_used, layer, w_gu, b_gu, w_dn, b_dn, *, tm, tf=512, tn=512):
    R, D = xs.shape
    L, E, _, F2 = w_gu.shape
    Fh = F2 // 2
    tf = _tile(Fh, tf)
    tn = _tile(D, tn)
    nb = R // tm
    nf = Fh // tf

    def xmap(j, m, be, nu):
        return (jnp.minimum(m, nu[0] - 1), 0)

    act = pl.pallas_call(
        _gmm1_body,
        grid_spec=pltpu.PrefetchScalarGridSpec(
            num_scalar_prefetch=2,
            grid=(nf, nb),
            in_specs=[pl.BlockSpec((tm, D), xmap),
                      pl.BlockSpec((1, 1, D, tf), lambda j, m, be, nu: (layer, be[m], 0, j)),
                      pl.BlockSpec((1, 1, D, tf), lambda j, m, be, nu: (layer, be[m], 0, nf + j)),
                      pl.BlockSpec((1, 1, 1, tf), lambda j, m, be, nu: (layer, be[m], 0, j)),
                      pl.BlockSpec((1, 1, 1, tf), lambda j, m, be, nu: (layer, be[m], 0, nf + j))],
            out_specs=pl.BlockSpec((tm, tf), lambda j, m, be, nu: (m, j)),
            scratch_shapes=[pltpu.VMEM((D, tf), MXU_DTYPE), pltpu.VMEM((D, tf), MXU_DTYPE)]),
        out_shape=jax.ShapeDtypeStruct((R, Fh), MXU_DTYPE),
        compiler_params=_params("arbitrary", "arbitrary"),
        name="moe_gate_up",
    )(blk_e, n_used, xs, w_gu, w_gu, b_gu.reshape(L, E, 1, F2), b_gu.reshape(L, E, 1, F2))

    ys = pl.pallas_call(
        _gmm2_body,
        grid_spec=pltpu.PrefetchScalarGridSpec(
            num_scalar_prefetch=2,
            grid=(D // tn, nb),
            in_specs=[pl.BlockSpec((tm, Fh), xmap),
                      pl.BlockSpec((1, 1, Fh, tn), lambda j, m, be, nu: (layer, be[m], 0, j)),
                      pl.BlockSpec((1, 1, 1, tn), lambda j, m, be, nu: (layer, be[m], 0, j))],
            out_specs=pl.BlockSpec((tm, tn), lambda j, m, be, nu: (m, j)),
            scratch_shapes=[pltpu.VMEM((Fh, tn), MXU_DTYPE)]),
        out_shape=jax.ShapeDtypeStruct((R, D), F32),
        compiler_params=_params("arbitrary", "arbitrary"),
        name="moe_down",
    )(blk_e, n_used, act, w_dn, b_dn.reshape(L, E, 1, D))
    return ys


def _moe(h, layer, w_router, b_router, w_gu, b_gu, w_dn, b_dn, *, tm):
    N, D = h.shape
    E = w_router.shape[-1]
    wr = jnp.pad(w_router[layer], ((0, 0), (0, LANES - E)))
    br = jnp.pad(b_router[layer], (0, LANES - E)).reshape(1, LANES)
    logits = _mm(h, wr, [F32], tn=LANES, bias=br)[0][:, :E]
    top_logit, top_e = lax.top_k(logits, TOP_K)
    gate = jax.nn.softmax(top_logit, axis=-1)
    e_flat = top_e.reshape(-1)
    onehot = (e_flat[:, None] == jnp.arange(E, dtype=jnp.int32)[None, :]).astype(jnp.int32)
    rank = jnp.take_along_axis(jnp.cumsum(onehot, axis=0) - onehot, e_flat[:, None], axis=1)[:, 0]
    counts = jnp.sum(onehot, axis=0)
    padded = (counts + tm - 1) // tm * tm
    pend = jnp.cumsum(padded)
    dest = (pend - padded)[e_flat] + rank
    n_blocks = -(-N * TOP_K // tm) + E
    src_tok = jnp.zeros((n_blocks * tm,), jnp.int32).at[dest].set(jnp.arange(N * TOP_K, dtype=jnp.int32) // TOP_K)
    blk_e = jnp.minimum(jnp.searchsorted(pend, jnp.arange(n_blocks, dtype=jnp.int32) * tm, side='right'),
                        E - 1).astype(jnp.int32)
    n_used = (pend[-1] // tm).astype(jnp.int32).reshape(1)
    xs = h[src_tok]
    ys = _moe_experts(xs, blk_e, n_used, layer, w_gu, b_gu, w_dn, b_dn, tm=tm)
    return jnp.sum(ys[dest].reshape(N, TOP_K, D) * gate[..., None], axis=1)


def kernel(x_prompt, x_sample, c_prompt, c_sample, cache_fox_kv, cache_fox_logf, cache_dsa_kv, cache_dsa_kidx, cache_swa_kv, w_ada, b_ada, g_mix, g_ffn, w_in_ab, b_forget, w_out_ab, w_in_c, sinks_c, w_out_c, w_router, b_router, w_gu, b_gu, w_dn, b_dn, g_final):
    Bp, Tp, D = x_prompt.shape
    Bs, Ts, _ = x_sample.shape
    depth = w_ada.shape[0]
    past_len = cache_fox_kv.shape[2]
    h_fox = cache_fox_kv.shape[4]
    kv_dsa = cache_dsa_kv.shape[4]
    d_idx = cache_dsa_kidx.shape[3]
    window = cache_swa_kv.shape[2]
    kv_swa = cache_swa_kv.shape[4]
    h_swa = sinks_c.shape[1]
    h_dsa = w_out_ab.shape[1] // HEAD_DIM - h_fox
    fox_w, dq_w, dkv_w = h_fox * HEAD_DIM, h_dsa * HEAD_DIM, kv_dsa * HEAD_DIM
    h_idx = (w_in_ab.shape[2] - 3 * fox_w - h_fox - dq_w - 2 * dkv_w - d_idx) // (d_idx + 1)
    assert Bp == 1 and Ts == CHUNK and Tp % CHUNK == 0 and past_len % CHUNK == 0
    assert d_idx == HEAD_DIM and window == 2 * CHUNK and d_idx + h_fox + h_idx <= LANES
    Mp, Ms = Bp * Tp, Bs * Ts
    M = Mp + Ms
    NB = M // CHUNK
    hd_scale = HEAD_DIM ** -0.5

    pos = jnp.concatenate([jnp.arange(Tp, dtype=jnp.int32),
                           jnp.tile(past_len + jnp.arange(Ts, dtype=jnp.int32), Bs)])
    half = HEAD_DIM // 2
    inv = ROPE_THETA ** (-jnp.arange(half, dtype=F32) / half)
    ang = pos.astype(F32)[:, None] * inv
    cos, sin = jnp.cos(ang), jnp.sin(ang)
    rope = (jnp.concatenate([cos, cos, cos, cos], axis=1), jnp.concatenate([-sin, sin, -sin, sin], axis=1))

    n_c = Bp + Bs
    c_all = jnp.pad(jnp.concatenate([c_prompt, c_sample], axis=0), ((0, -n_c % 8), (0, 0)))
    mod = _ada(c_all, w_ada, b_ada)
    blk_row = jnp.concatenate([jnp.zeros((Mp // CHUNK,), jnp.int32),
                               Bp + jnp.arange(Bs, dtype=jnp.int32)])
    x3 = jnp.concatenate([x_prompt.reshape(Mp // CHUNK, CHUNK, D), x_sample.reshape(Ms // CHUNK, CHUNK, D)], axis=0)

    fox_kv_p, fox_kv_s, logf_p, logf_s, dsa_kv_p, dsa_kv_s, kidx_p, kidx_s, swa_p, swa_s = ([] for _ in range(10))
    moe_tm = 256 if M * TOP_K >= 8192 else 64

    for l in range(depth):
        mod_exp = mod[l][blk_row][:, None, :]
        h = _norm_mod(x3, g_mix[l], mod_exp, 0, 1).reshape(M, D)
        j = l // 2
        if l % 2 == 0:
            w = w_in_ab[j]
            offs = np.cumsum([0, fox_w, fox_w, fox_w, h_fox, dq_w, dkv_w, dkv_w, h_idx * d_idx, d_idx, h_idx])
            seg = lambda a, b: w[:, offs[a]:offs[b]].astype(MXU_DTYPE)
            q_fox = _mm(h, seg(0, 1), [MXU_DTYPE], scale=hd_scale)[0]
            fkv32, fkv = _mm(h, seg(1, 3), [F32, MXU_DTYPE])
            q_dsa = _mm(h, seg(4, 5), [MXU_DTYPE], scale=hd_scale, rope=rope, rope_tiles=10 ** 6)[0]
            dkv32 = _mm(h, seg(5, 7), [F32], tn=dkv_w, rope=rope, rope_tiles=1)[0]
            qi = _mm(h, seg(7, 8), [MXU_DTYPE], scale=d_idx ** -0.5, rope=rope, rope_tiles=10 ** 6)[0]
            pad = LANES - d_idx - h_fox - h_idx
            w_small = jnp.concatenate([w[:, offs[8]:offs[9]], w[:, offs[3]:offs[4]], w[:, offs[9]:offs[10]],
                                       jnp.zeros((D, pad), w.dtype)], axis=1).astype(MXU_DTYPE)
            bf_row = jnp.pad(b_forget[j], (d_idx, LANES - d_idx - h_fox)).reshape(1, LANES)
            small = _mm_small(h, w_small, rope, bf_row, d_idx=d_idx, h_fox=h_fox, wi_scale=h_idx ** -0.5)
            kidx, logf, wi = small[:, :d_idx], small[:, d_idx:d_idx + h_fox], small[:, d_idx + h_fox:d_idx + h_fox + h_idx]

            fox_kv_p.append(fkv32[:Mp].reshape(Bp, Tp, 2, h_fox, HEAD_DIM))
            fox_kv_s.append(fkv32[Mp:].reshape(Bs, Ts, 2, h_fox, HEAD_DIM))
            logf_p.append(logf[:Mp].reshape(Bp, Tp, h_fox))
            logf_s.append(logf[Mp:].reshape(Bs, Ts, h_fox))
            dsa_kv_p.append(dkv32[:Mp].reshape(Bp, Tp, 2, kv_dsa, HEAD_DIM))
            dsa_kv_s.append(dkv32[Mp:].reshape(Bs, Ts, 2, kv_dsa, HEAD_DIM))
            kidx_p.append(kidx[:Mp].reshape(Bp, Tp, d_idx))
            kidx_s.append(kidx[Mp:].reshape(Bs, Ts, d_idx))

            cum_p = jnp.cumsum(logf_p[-1], axis=1)
            oa_p = _fox_attention(q_fox[:Mp].reshape(Bp, Tp, fox_w), fkv[:Mp].reshape(Bp, Tp, 2 * fox_w),
                                  cum_p, cum_p, tq=min(256, Tp), tk=min(256, Tp))
            kv_s = jnp.concatenate([cache_fox_kv[j].reshape(Bs, past_len, 2 * fox_w).astype(MXU_DTYPE),
                                    fkv[Mp:].reshape(Bs, Ts, 2 * fox_w)], axis=1)
            cum_s = jnp.cumsum(jnp.concatenate([cache_fox_logf[j].astype(F32), logf_s[-1]], axis=1), axis=1)
            oa_s = _fox_attention(q_fox[Mp:].reshape(Bs, Ts, fox_w), kv_s, cum_s[:, past_len:], cum_s,
                                  tq=Ts, tk=_tile(past_len, 256, CHUNK))
            dkv = dkv32.astype(MXU_DTYPE)
            kid = jnp.concatenate([kidx, kidx], axis=1).astype(MXU_DTYPE)
            ob_p = _dsa_attention(q_dsa[:Mp].reshape(Bp, Tp, dq_w), dkv[:Mp, :dkv_w].reshape(Bp, Tp, dkv_w),
                                  dkv[:Mp, dkv_w:].reshape(Bp, Tp, dkv_w), qi[:Mp].reshape(Bp, Tp, -1),
                                  kid[:Mp].reshape(Bp, Tp, LANES), wi[:Mp].reshape(Bp, Tp, h_idx),
                                  s_real=Tp, topk=min(TOPK_MAX, Tp // 4), tq=min(128, Tp), tk=min(512, Tp))
            S = past_len + Ts
            s_pad = -(-S // LANES) * LANES
            padk = lambda a: jnp.pad(a, ((0, 0), (0, s_pad - S), (0, 0)))
            pkv = cache_dsa_kv[j].reshape(Bs, past_len, 2 * dkv_w).astype(MXU_DTYPE)
            k_s = padk(jnp.concatenate([pkv[:, :, :dkv_w], dkv[Mp:, :dkv_w].reshape(Bs, Ts, dkv_w)], axis=1))
            v_s = padk(jnp.concatenate([pkv[:, :, dkv_w:], dkv[Mp:, dkv_w:].reshape(Bs, Ts, dkv_w)], axis=1))
            pki = cache_dsa_kidx[j].astype(MXU_DTYPE)
            kid_s = padk(jnp.concatenate([jnp.concatenate([pki, pki], axis=2), kid[Mp:].reshape(Bs, Ts, LANES)], axis=1))
            ob_s = _dsa_attention(q_dsa[Mp:].reshape(Bs, Ts, dq_w), k_s, v_s, qi[Mp:].reshape(Bs, Ts, -1),
                                  kid_s, wi[Mp:].reshape(Bs, Ts, h_idx),
                                  s_real=S, topk=min(TOPK_MAX, S // 4), tq=Ts, tk=LANES)
            oa = jnp.concatenate([oa_p.reshape(Mp, fox_w), oa_s.reshape(Ms, fox_w)], axis=0)
            ob = jnp.concatenate([ob_p.reshape(Mp, dq_w), ob_s.reshape(Ms, dq_w)], axis=0)
            if fox_w == dq_w:
                x3 = _mm_res([oa, ob], w_out_ab, j, x3, mod_exp, 2)
            else:
                x3 = _mm_res([jnp.concatenate([oa, ob], axis=1)], w_out_ab, j, x3, mod_exp, 2)
        else:
            w = w_in_c[j]
            qw, kw = h_swa * HEAD_DIM, kv_swa * HEAD_DIM
            q = _mm(h, w[:, :qw].astype(MXU_DTYPE), [MXU_DTYPE], scale=hd_scale, rope=rope, rope_tiles=10 ** 6)[0]
            kv32 = _mm(h, w[:, qw:].astype(MXU_DTYPE), [F32], tn=kw, rope=rope, rope_tiles=1)[0]
            kvb = kv32.astype(MXU_DTYPE)
            kv_new_p = kv32[:Mp].reshape(Bp, Tp, 2, kv_swa, HEAD_DIM)
            kv_new_s = kv32[Mp:].reshape(Bs, Ts, 2, kv_swa, HEAD_DIM)
            swa_p.append(kv_new_p[:, Tp - min(window, Tp):])
            swa_s.append(jnp.concatenate([cache_swa_kv[j], kv_new_s], axis=1)[:, Ts:])
            zpad = jnp.zeros((Bp, window, kw), MXU_DTYPE)
            o_p = _swa_attention(q[:Mp].reshape(Bp, Tp, qw),
                                 jnp.concatenate([zpad, kvb[:Mp, :kw].reshape(Bp, Tp, kw)], axis=1),
                                 jnp.concatenate([zpad, kvb[:Mp, kw:].reshape(Bp, Tp, kw)], axis=1),
                                 sinks_c[j], first_valid=window)
            past = cache_swa_kv[j].reshape(Bs, window, 2 * kw).astype(MXU_DTYPE)
            o_s = _swa_attention(q[Mp:].reshape(Bs, Ts, qw),
                                 jnp.concatenate([past[:, :, :kw], kvb[Mp:, :kw].reshape(Bs, Ts, kw)], axis=1),
                                 jnp.concatenate([past[:, :, kw:], kvb[Mp:, kw:].reshape(Bs, Ts, kw)], axis=1),
                                 sinks_c[j], first_valid=0)
            o = jnp.concatenate([o_p.reshape(Mp, qw), o_s.reshape(Ms, qw)], axis=0)
            x3 = _mm_res([o], w_out_c, j, x3, mod_exp, 2)

        h2 = _norm_mod(x3, g_ffn[l], mod_exp, 3, 4).reshape(M, D)
        y = _moe(h2, l, w_router, b_router, w_gu, b_gu, w_dn, b_dn, tm=moe_tm)
        x3 = x3 + mod_exp[:, :, 5 * D:] * y.reshape(NB, CHUNK, D)

    y = _final_norm(x3, g_final).reshape(M, D)
    st = lambda xs: jnp.stack(xs, axis=0)
    return (y[:Mp].reshape(Bp, Tp, D), y[Mp:].reshape(Bs, Ts, D),
            st(fox_kv_p), st(fox_kv_s), st(logf_p), st(logf_s), st(dsa_kv_p), st(dsa_kv_s),
            st(kidx_p), st(kidx_s), st(swa_p), st(swa_s))
```

```python
import functools

import numpy as np
import jax
import jax.numpy as jnp
from jax import lax
from jax.experimental import pallas as pl
from jax.experimental.pallas import tpu as pltpu

CHUNK = 64
HEAD_DIM = 64
ROPE_THETA = 10000.0
EPS = 1e-5
NEG = -1e30
TOPK_MAX = 256
TOP_K = 4
SWIGLU_ALPHA = 1.702
SWIGLU_LIMIT = 7.0

LANES = 128
F32 = jnp.float32
MXU_DTYPE = jnp.bfloat16
VMEM_LIMIT = 56 * 1024 * 1024

INT_MIN = -2 ** 31
_negbits = int(np.array(NEG, np.float32).view(np.int32))
KEY_NEG = _negbits ^ 0x7FFFFFFF

_NT = (((1,), (1,)), ((), ()))


def _params(*sem):
    return pltpu.CompilerParams(dimension_semantics=sem, vmem_limit_bytes=VMEM_LIMIT)


def _tile(n, pref, mult=LANES):
    if n <= pref:
        return n
    t = pref - pref % mult
    while t >= mult:
        if n % t == 0:
            return t
        t -= mult
    return n


def _sigmoid(x):
    return 1.0 / (1.0 + jnp.exp(-x))


def _ada_body(c_ref, w_ref, b_ref, o_ref):
    c = c_ref[...]
    a = (c * _sigmoid(c)).astype(MXU_DTYPE)
    o_ref[0] = jnp.dot(a, w_ref[0].astype(MXU_DTYPE), preferred_element_type=F32) + b_ref[0]


def _ada(c_all, w_ada, b_ada):
    L, D, N = w_ada.shape
    R = c_all.shape[0]
    tn = _tile(N, 1024)
    return pl.pallas_call(
        _ada_body,
        grid=(L, N // tn),
        in_specs=[pl.BlockSpec((R, D), lambda l, j: (0, 0)),
                  pl.BlockSpec((1, D, tn), lambda l, j: (l, 0, j)),
                  pl.BlockSpec((1, 1, tn), lambda l, j: (l, 0, j))],
        out_specs=pl.BlockSpec((1, R, tn), lambda l, j: (l, 0, j)),
        out_shape=jax.ShapeDtypeStruct((L, R, N), F32),
        compiler_params=_params("arbitrary", "arbitrary"),
        name="ada_mod",
    )(c_all, w_ada, b_ada.reshape(L, 1, N))


def _norm_mod_body(x_ref, g_ref, sh_ref, sc_ref, o_ref):
    x = x_ref[...]
    ms = jnp.mean(x * x, axis=-1, keepdims=True)
    xn = x * lax.rsqrt(ms + EPS) * g_ref[...]
    o_ref[...] = (xn * (1.0 + sc_ref[...]) + sh_ref[...]).astype(o_ref.dtype)


def _norm_body(x_ref, g_ref, o_ref):
    x = x_ref[...]
    ms = jnp.mean(x * x, axis=-1, keepdims=True)
    o_ref[...] = (x * lax.rsqrt(ms + EPS) * g_ref[...]).astype(o_ref.dtype)


def _pack_pairs(h):
    half = h.shape[-1] // 2
    bits = pltpu.bitcast(h.astype(jnp.bfloat16).astype(F32), jnp.int32)
    return bits[..., half:] | lax.shift_right_logical(bits[..., :half], 16)


def _unpack_pairs(w):
    lo = pltpu.bitcast(lax.shift_left(w, 16), F32).astype(jnp.bfloat16)
    hi = pltpu.bitcast(w & jnp.int32(-65536), F32).astype(jnp.bfloat16)
    return jnp.concatenate([lo, hi], axis=1)


def _norm_mod_pack_body(x_ref, g_ref, sh_ref, sc_ref, o_ref, p_ref):
    x = x_ref[...]
    ms = jnp.mean(x * x, axis=-1, keepdims=True)
    xn = x * lax.rsqrt(ms + EPS) * g_ref[...]
    h = xn * (1.0 + sc_ref[...]) + sh_ref[...]
    o_ref[...] = h.astype(o_ref.dtype)
    p_ref[...] = _pack_pairs(h) if p_ref.dtype == jnp.int32 else h


def _norm_mod(x3, g, mod_exp, k_shift, k_scale, packed=False):
    NB, C, D = x3.shape
    bb = _tile(NB, 8, 1)
    blk = pl.BlockSpec((bb, C, D), lambda i: (i, 0, 0))
    out_specs, out_shape, body = blk, jax.ShapeDtypeStruct((NB, C, D), MXU_DTYPE), _norm_mod_body
    if packed:
        body = _norm_mod_pack_body
        if MXU_DTYPE == jnp.bfloat16:
            out_specs = [blk, pl.BlockSpec((bb, C, D // 2), lambda i: (i, 0, 0))]
            out_shape = [out_shape, jax.ShapeDtypeStruct((NB, C, D // 2), jnp.int32)]
        else:
            out_specs = [blk, blk]
            out_shape = [out_shape, jax.ShapeDtypeStruct((NB, C, D), F32)]
    return pl.pallas_call(
        body,
        grid=(NB // bb,),
        in_specs=[blk,
                  pl.BlockSpec((1, 1, D), lambda i: (0, 0, 0)),
                  pl.BlockSpec((bb, 1, D), lambda i: (i, 0, k_shift)),
                  pl.BlockSpec((bb, 1, D), lambda i: (i, 0, k_scale))],
        out_specs=out_specs,
        out_shape=out_shape,
        compiler_params=_params("arbitrary"),
        name="norm_mod",
    )(x3, g.reshape(1, 1, D), mod_exp, mod_exp)


def _final_norm(x3, g):
    NB, C, D = x3.shape
    bb = _tile(NB, 8, 1)
    return pl.pallas_call(
        _norm_body,
        grid=(NB // bb,),
        in_specs=[pl.BlockSpec((bb, C, D), lambda i: (i, 0, 0)),
                  pl.BlockSpec((1, 1, D), lambda i: (0, 0, 0))],
        out_specs=pl.BlockSpec((bb, C, D), lambda i: (i, 0, 0)),
        out_shape=jax.ShapeDtypeStruct((NB, C, D), F32),
        compiler_params=_params("arbitrary"),
        name="final_norm",
    )(x3, g.reshape(1, 1, D))


def _rope_tile(acc, cos, sin):
    tn = acc.shape[1]
    reps = tn // LANES
    cosf = jnp.concatenate([cos] * reps, axis=1) if reps > 1 else cos
    sinf = jnp.concatenate([sin] * reps, axis=1) if reps > 1 else sin
    lane = lax.broadcasted_iota(jnp.int32, acc.shape, 1)
    first_half = (lane & (HEAD_DIM - 1)) < HEAD_DIM // 2
    rot = jnp.where(first_half, pltpu.roll(acc, tn - HEAD_DIM // 2, 1), pltpu.roll(acc, HEAD_DIM // 2, 1))
    return acc * cosf + rot * sinf


def _mm_body(*refs, scale, rope_tiles, n_col_tiles, has_rope, has_bias):
    x_ref, w_ref = refs[:2]
    idx = 2
    if has_rope:
        cos_ref, sin_ref = refs[idx:idx + 2]
        idx += 2
    if has_bias:
        b_ref = refs[idx]
        idx += 1
    out_refs = refs[idx:]
    acc = jnp.dot(x_ref[...], w_ref[...].astype(MXU_DTYPE), preferred_element_type=F32)
    if has_bias:
        acc = acc + b_ref[...]

    def emit(val):
        if scale != 1.0:
            val = val * scale
        for o in out_refs:
            o[...] = val.astype(o.dtype)

    if has_rope and rope_tiles > 0:
        if rope_tiles >= n_col_tiles:
            emit(_rope_tile(acc, cos_ref[...], sin_ref[...]))
        else:
            j = pl.program_id(1)

            @pl.when(j < rope_tiles)
            def _():
                emit(_rope_tile(acc, cos_ref[...], sin_ref[...]))

            @pl.when(j >= rope_tiles)
            def _():
                emit(acc)
    else:
        emit(acc)


def _mm(x, w, out_dtypes, *, tm=1024, tn=512, scale=1.0, rope=None, rope_tiles=0, bias=None):
    M, K = x.shape
    N = w.shape[1]
    tm = _tile(M, tm, 64)
    tn = _tile(N, tn)
    n_col = N // tn
    in_specs = [pl.BlockSpec((tm, K), lambda i, j: (i, 0)),
                pl.BlockSpec((K, tn), lambda i, j: (0, j))]
    args = [x, w]
    if rope is not None:
        in_specs += [pl.BlockSpec((tm, LANES), lambda i, j: (i, 0))] * 2
        args += list(rope)
    if bias is not None:
        in_specs.append(pl.BlockSpec((1, tn), lambda i, j: (0, j)))
        args.append(bias)
    outs = pl.pallas_call(
        functools.partial(_mm_body, scale=scale, rope_tiles=rope_tiles if rope is not None else 0,
                          n_col_tiles=n_col, has_rope=rope is not None, has_bias=bias is not None),
        grid=(M // tm, n_col),
        in_specs=in_specs,
        out_specs=[pl.BlockSpec((tm, tn), lambda i, j: (i, j)) for _ in out_dtypes],
        out_shape=[jax.ShapeDtypeStruct((M, N), dt) for dt in out_dtypes],
        compiler_params=_params("arbitrary", "arbitrary"),
        name="proj",
    )(*args)
    return outs


def _small_body(x_ref, w_ref, cos_ref, sin_ref, bf_ref, o_ref, *, d_idx, h_fox, wi_scale):
    acc = jnp.dot(x_ref[...], w_ref[...].astype(MXU_DTYPE), preferred_element_type=F32)
    roped = _rope_tile(acc, cos_ref[...], sin_ref[...])
    z = acc + bf_ref[...]
    logf = jnp.minimum(z, 0.0) - jnp.log1p(jnp.exp(-jnp.abs(z)))
    lane = lax.broadcasted_iota(jnp.int32, acc.shape, 1)
    o_ref[...] = jnp.where(lane < d_idx, roped, jnp.where(lane < d_idx + h_fox, logf, acc * wi_scale))


def _mm_small(x, w, rope, bf_row, *, d_idx, h_fox, wi_scale, tm=1024):
    M, K = x.shape
    tm = _tile(M, tm, 64)
    return pl.pallas_call(
        functools.partial(_small_body, d_idx=d_idx, h_fox=h_fox, wi_scale=wi_scale),
        grid=(M // tm,),
        in_specs=[pl.BlockSpec((tm, K), lambda i: (i, 0)),
                  pl.BlockSpec((K, LANES), lambda i: (0, 0)),
                  pl.BlockSpec((tm, LANES), lambda i: (i, 0)),
                  pl.BlockSpec((tm, LANES), lambda i: (i, 0)),
                  pl.BlockSpec((1, LANES), lambda i: (0, 0))],
        out_specs=pl.BlockSpec((tm, LANES), lambda i: (i, 0)),
        out_shape=jax.ShapeDtypeStruct((M, LANES), F32),
        compiler_params=_params("arbitrary"),
        name="proj_small",
    )(x, w, rope[0], rope[1], bf_row)


def _mm_res_body(*refs, n_pairs):
    xs = refs[:n_pairs]
    ws = refs[n_pairs:2 * n_pairs]
    res_ref, gate_ref, o_ref = refs[2 * n_pairs:]
    acc = None
    for x_ref, w_ref in zip(xs, ws):
        part = jnp.dot(x_ref[...], w_ref[0].astype(MXU_DTYPE), preferred_element_type=F32)
        acc = part if acc is None else acc + part
    tm, tn = acc.shape
    o_ref[...] = res_ref[...] + gate_ref[...] * acc.reshape(tm // CHUNK, CHUNK, tn)


def _mm_res(xs, w_all, layer, res3, mod_exp, k_gate, *, tm=1024, tn=512):
    NB, C, N = res3.shape
    M = NB * C
    tm = _tile(M, tm, C)
    tn = _tile(N, tn)
    n_col = N // tn
    kp = xs[0].shape[1]
    assert all(x.shape == (M, kp) for x in xs) and w_all.shape[1] == kp * len(xs)
    in_specs = [pl.BlockSpec((tm, kp), lambda i, j: (i, 0)) for _ in xs]
    in_specs += [pl.BlockSpec((1, kp, tn), functools.partial(lambda i, j, p: (layer, p, j), p=p))
                 for p in range(len(xs))]
    in_specs += [pl.BlockSpec((tm // C, C, tn), lambda i, j: (i, 0, j)),
                 pl.BlockSpec((tm // C, 1, tn), lambda i, j: (i, 0, k_gate * n_col + j))]
    return pl.pallas_call(
        functools.partial(_mm_res_body, n_pairs=len(xs)),
        grid=(M // tm, n_col),
        in_specs=in_specs,
        out_specs=pl.BlockSpec((tm // C, C, tn), lambda i, j: (i, 0, j)),
        out_shape=jax.ShapeDtypeStruct((NB, C, N), F32),
        compiler_params=_params("arbitrary", "arbitrary"),
        name="out_proj_res",
    )(*xs, *([w_all] * len(xs)), res3, mod_exp)


def _fox_body(q_ref, k_ref, v_ref, cq_ref, ck_ref, o_ref, *, tq, tk, q_off, nq):
    q0 = q_off if nq == 1 else q_off + pl.program_id(2) * tq
    n_full = q0 // tk
    q = q_ref[0]
    lo = lax.broadcasted_iota(jnp.int32, (tq, LANES), 1) < HEAD_DIM
    zero = jnp.zeros_like(q)
    qh = (jnp.where(lo, q, zero), jnp.where(lo, zero, q))
    cq = cq_ref[0, 0]
    cqh = (cq[:, 0:1], cq[:, 1:2])

    def chunk(c, size, carry, masked):
        k0 = c * tk
        if not isinstance(k0, int):
            k0 = pl.multiple_of(k0, tk)
        kc = k_ref[0, pl.ds(k0, size), :]
        vc = v_ref[0, pl.ds(k0, size), :]
        out = []
        for h in range(2):
            m, l, acc = carry[3 * h:3 * h + 3]
            s = lax.dot_general(qh[h], kc, _NT, preferred_element_type=F32)
            ck = ck_ref[0, 0, h, pl.ds(c, 1), :][:, :size]
            s = s + (cqh[h] - ck)
            if masked:
                row = lax.broadcasted_iota(jnp.int32, s.shape, 0)
                col = lax.broadcasted_iota(jnp.int32, s.shape, 1)
                s = jnp.where(col <= row, s, NEG)
            m_new = jnp.maximum(m, jnp.max(s, axis=1, keepdims=True))
            alpha = jnp.exp(m - m_new)
            p = jnp.exp(s - m_new)
            l = alpha * l + jnp.sum(p, axis=1, keepdims=True)
            acc = alpha * acc + jnp.dot(p.astype(MXU_DTYPE), vc, preferred_element_type=F32)
            out += [m_new, l, acc]
        return tuple(out)

    init = (jnp.full((tq, 1), NEG, F32), jnp.zeros((tq, 1), F32), jnp.zeros((tq, LANES), F32)) * 2
    carry = lax.fori_loop(0, n_full, lambda c, cr: chunk(c, tk, cr, False), init)
    carry = chunk(n_full, tq, carry, True)
    o = jnp.where(lo, carry[2] / carry[1], carry[5] / carry[4])
    o_ref[0] = o.astype(o_ref.dtype)


def _fox_attention(q, kv, cum_q, cum_k, *, tq, tk):
    B, T, HD = q.shape
    S = kv.shape[1]
    n_pairs = HD // LANES
    q_off = S - T
    nq = T // tq
    assert T % tq == 0 and q_off % tk == 0 and (tq == tk or (nq == 1 and tq <= tk))
    s_pad = -(-S // tk) * tk
    cq = cum_q.reshape(B, T, n_pairs, 2).transpose(0, 2, 1, 3)
    ck = jnp.pad(cum_k, ((0, 0), (0, s_pad - S), (0, 0)))
    ck = ck.reshape(B, s_pad // tk, tk, n_pairs, 2).transpose(0, 3, 4, 1, 2)
    return pl.pallas_call(
        functools.partial(_fox_body, tq=tq, tk=tk, q_off=q_off, nq=nq),
        grid=(B, n_pairs, nq),
        in_specs=[pl.BlockSpec((1, tq, LANES), lambda b, p, i: (b, i, p)),
                  pl.BlockSpec((1, S, LANES), lambda b, p, i: (b, 0, p)),
                  pl.BlockSpec((1, S, LANES), lambda b, p, i: (b, 0, n_pairs + p)),
                  pl.BlockSpec((1, 1, tq, 2), lambda b, p, i: (b, p, i, 0)),
                  pl.BlockSpec((1, 1, 2, s_pad // tk, tk), lambda b, p, i: (b, p, 0, 0, 0))],
        out_specs=pl.BlockSpec((1, tq, LANES), lambda b, p, i: (b, i, p)),
        out_shape=jax.ShapeDtypeStruct((B, T, HD), MXU_DTYPE),
        compiler_params=_params("arbitrary", "arbitrary", "arbitrary"),
        name="fox_attention",
    )(q, kv, kv, cq, ck)


def _swap_halves(x):
    return jnp.concatenate([x[:, HEAD_DIM:], x[:, :HEAD_DIM]], axis=1)


def _dsa_body(q_ref, k_ref, v_ref, qi_ref, kid_ref, wi_ref, o_ref,
              keys_scr, bias_scr, m_scr, l_scr, acc_scr, j_scr, *,
              tq, tk, q_off, s_real, topk, n_heads, n_kv, n_idx, nq, n_chunks):
    if nq == 1:
        q0 = q_off
        n_ch = min((q0 + tq + tk - 1) // tk, n_chunks)
        unscanned = float(s_real - min(n_ch * tk, s_real))
    else:
        q0 = q_off + pl.program_id(1) * tq
        n_ch = jnp.minimum((q0 + tq + tk - 1) // tk, n_chunks)
        unscanned = (s_real - jnp.minimum(n_ch * tk, s_real)).astype(F32)
    row = lax.broadcasted_iota(jnp.int32, (tq, tk), 0)
    col = lax.broadcasted_iota(jnp.int32, (tq, tk), 1)
    qchunk = (q0 + row) // CHUNK
    lo = lax.broadcasted_iota(jnp.int32, (tq, LANES), 1) < HEAD_DIM

    def admissible(kpos):
        return ((kpos // CHUNK) <= qchunk) & (kpos < s_real)

    qi = qi_ref[0]
    qim = []
    for p in range(n_idx // 2):
        blk = qi[:, p * LANES:(p + 1) * LANES]
        zero = jnp.zeros_like(blk)
        qim += [jnp.where(lo, blk, zero), jnp.where(lo, zero, blk)]
    wi = wi_ref[0]
    wcols = [wi[:, h:h + 1] for h in range(n_idx)]

    def score_chunk(c, _):
        kc = kid_ref[0, pl.ds(pl.multiple_of(c * tk, tk), tk), :]
        score = jnp.zeros((tq, tk), F32)
        for h in range(n_idx):
            s = lax.dot_general(qim[h], kc, _NT, preferred_element_type=F32)
            score = score + wcols[h] * jnp.maximum(s, 0.0)
        kpos = c * tk + col
        sm = jnp.where(admissible(kpos), score, NEG)
        bits = pltpu.bitcast(sm, jnp.int32)
        key = jnp.where(bits < 0, bits ^ 0x7FFFFFFF, bits)
        key = jnp.where(bits == INT_MIN, 0, key)
        keys_scr[c] = jnp.where(kpos < s_real, key, INT_MIN)
        return 0

    lax.fori_loop(0, n_ch, score_chunk, 0)

    def count(pred):
        def body(c, acc):
            x = jnp.where(pred(keys_scr[c], c), 1.0, 0.0)
            for j in range(tk // LANES):
                acc = acc + x[:, j * LANES:(j + 1) * LANES]
            return acc
        acc = lax.fori_loop(0, n_ch, body, jnp.zeros((tq, LANES), F32))
        return jnp.sum(acc, axis=1, keepdims=True)

    kf = float(topk)

    def bit_step(it, carry):
        t, cge = carry
        cand = t + jnp.left_shift(jnp.int32(1), 31 - it)
        cnt = count(lambda kc, c: kc >= cand) + jnp.where(cand <= KEY_NEG, unscanned, 0.0)
        ok = cnt >= kf
        return jnp.where(ok, cand, t), jnp.where(ok, cnt, cge)

    thr, cge = lax.fori_loop(0, 32, bit_step,
                             (jnp.full((tq, 1), INT_MIN, jnp.int32), jnp.full((tq, 1), 3e38, F32)))
    cgt = count(lambda kc, c: kc > thr) + jnp.where(thr < KEY_NEG, unscanned, 0.0)
    need = kf - cgt

    n_bits = int(n_chunks * tk).bit_length()
    j_scr[...] = jnp.full((tq, 1), 2 ** n_bits, jnp.int32)

    @pl.when(jnp.max(cge) > kf)
    def _():
        def j_step(it, jcur):
            cand = jcur + jnp.left_shift(jnp.int32(1), n_bits - 1 - it)
            f = count(lambda kc, c: (kc == thr) & ((c * tk + col) < cand))
            return jnp.where(f < need, cand, jcur)
        j_scr[...] = lax.fori_loop(0, n_bits, j_step, jnp.zeros((tq, 1), jnp.int32))

    jmax = j_scr[...]

    def bias_chunk(c, _):
        kc = keys_scr[c]
        kpos = c * tk + col
        sel = (kc > thr) | ((kc == thr) & (kpos <= jmax))
        bias_scr[c] = jnp.where(sel & admissible(kpos), 0.0, NEG)
        return 0

    lax.fori_loop(0, n_ch, bias_chunk, 0)

    q = q_ref[0]
    G = n_heads // n_kv
    outs = []
    for g in range(n_kv):
        r, kh = g // 2, g % 2
        qms = []
        for jh in range(G):
            h = g * G + jh
            blk = q[:, (h // 2) * LANES:(h // 2 + 1) * LANES]
            if h % 2 != kh:
                blk = _swap_halves(blk)
            zero = jnp.zeros_like(blk)
            qms.append(jnp.where(lo, blk, zero) if kh == 0 else jnp.where(lo, zero, blk))
        m_scr[...] = jnp.full(m_scr.shape, NEG, F32)
        l_scr[...] = jnp.zeros(l_scr.shape, F32)
        acc_scr[...] = jnp.zeros(acc_scr.shape, F32)

        def attend(c, _):
            k0 = pl.multiple_of(c * tk, tk)
            kc = k_ref[0, pl.ds(k0, tk), r * LANES:(r + 1) * LANES]
            vc = v_ref[0, pl.ds(k0, tk), r * LANES:(r + 1) * LANES]
            b = bias_scr[c]
            for jh in range(G):
                s = lax.dot_general(qms[jh], kc, _NT, preferred_element_type=F32) + b
                m = m_scr[jh]
                m_new = jnp.maximum(m, jnp.max(s, axis=1, keepdims=True))
                alpha = jnp.exp(m - m_new)
                p = jnp.exp(s - m_new)
                l_scr[jh] = alpha * l_scr[jh] + jnp.sum(p, axis=1, keepdims=True)
                acc_scr[jh] = alpha * acc_scr[jh] + jnp.dot(p.astype(MXU_DTYPE), vc, preferred_element_type=F32)
                m_scr[jh] = m_new
            return 0

        lax.fori_loop(0, n_ch, attend, 0)
        for jh in range(G):
            outs.append((acc_scr[jh] / l_scr[jh], kh))

    for pr in range(n_heads // 2):
        (o0, h0), (o1, h1) = outs[2 * pr], outs[2 * pr + 1]
        a = o0 if h0 == 0 else pltpu.roll(o0, HEAD_DIM, 1)
        b = o1 if h1 == 1 else pltpu.roll(o1, HEAD_DIM, 1)
        o_ref[0, :, pr * LANES:(pr + 1) * LANES] = jnp.where(lo, a, b).astype(o_ref.dtype)


def _dsa_attention(q, k, v, qi, kid, wi, *, s_real, topk, tq, tk):
    B, T, HD = q.shape
    s_pad = k.shape[1]
    n_heads, n_kv, n_idx = HD // HEAD_DIM, k.shape[2] // HEAD_DIM, qi.shape[2] // HEAD_DIM
    assert n_kv % 2 == 0 and n_heads % n_kv == 0 and n_idx % 2 == 0 and s_pad % tk == 0 and T % tq == 0
    G = n_heads // n_kv
    nq = T // tq
    n_chunks = s_pad // tk
    body = functools.partial(_dsa_body, tq=tq, tk=tk, q_off=s_real - T, s_real=s_real, topk=topk,
                             n_heads=n_heads, n_kv=n_kv, n_idx=n_idx, nq=nq, n_chunks=n_chunks)
    return pl.pallas_call(
        body,
        grid=(B, nq),
        in_specs=[pl.BlockSpec((1, tq, HD), lambda b, i: (b, i, 0)),
                  pl.BlockSpec((1, s_pad, k.shape[2]), lambda b, i: (b, 0, 0)),
                  pl.BlockSpec((1, s_pad, v.shape[2]), lambda b, i: (b, 0, 0)),
                  pl.BlockSpec((1, tq, qi.shape[2]), lambda b, i: (b, i, 0)),
                  pl.BlockSpec((1, s_pad, LANES), lambda b, i: (b, 0, 0)),
                  pl.BlockSpec((1, tq, n_idx), lambda b, i: (b, i, 0))],
        out_specs=pl.BlockSpec((1, tq, HD), lambda b, i: (b, i, 0)),
        out_shape=jax.ShapeDtypeStruct((B, T, HD), MXU_DTYPE),
        scratch_shapes=[pltpu.VMEM((n_chunks, tq, tk), jnp.int32),
                        pltpu.VMEM((n_chunks, tq, tk), F32),
                        pltpu.VMEM((G, tq, 1), F32),
                        pltpu.VMEM((G, tq, 1), F32),
                        pltpu.VMEM((G, tq, LANES), F32),
                        pltpu.VMEM((tq, 1), jnp.int32)],
        compiler_params=_params("arbitrary", "arbitrary"),
        name="dsa_attention",
    )(q, k, v, qi, kid, wi)


def _swa_body(sinks_ref, q_ref, k0_ref, k1_ref, k2_ref, v0_ref, v1_ref, v2_ref, o_ref, *,
              first_valid, n_heads, n_kv):
    c = pl.program_id(1)
    q = q_ref[0]
    k = jnp.concatenate([k0_ref[0], k1_ref[0], k2_ref[0]], axis=0)
    v = jnp.concatenate([v0_ref[0], v1_ref[0], v2_ref[0]], axis=0)
    n_keys = k.shape[0]
    G = n_heads // n_kv
    valid = (c * CHUNK + lax.broadcasted_iota(jnp.int32, (G * CHUNK, n_keys), 1)) >= first_valid
    pieces = []
    for g in range(n_kv):
        qs = jnp.concatenate([q[:, h * HEAD_DIM:(h + 1) * HEAD_DIM] for h in range(g * G, (g + 1) * G)], axis=0)
        kg = k[:, g * HEAD_DIM:(g + 1) * HEAD_DIM]
        vg = v[:, g * HEAD_DIM:(g + 1) * HEAD_DIM]
        sink = jnp.concatenate([jnp.full((CHUNK, 1), sinks_ref[h], F32) for h in range(g * G, (g + 1) * G)], axis=0)
        s = lax.dot_general(qs, kg, _NT, preferred_element_type=F32)
        s = jnp.where(valid, s, NEG)
        m = jnp.maximum(jnp.max(s, axis=1, keepdims=True), sink)
        e = jnp.exp(s - m)
        p = e / (jnp.sum(e, axis=1, keepdims=True) + jnp.exp(sink - m))
        o = jnp.dot(p.astype(MXU_DTYPE), vg, preferred_element_type=F32)
        pieces += [o[jh * CHUNK:(jh + 1) * CHUNK] for jh in range(G)]
    o_ref[0] = jnp.concatenate(pieces, axis=1).astype(o_ref.dtype)


def _swa_attention(q, kpad, vpad, sinks, *, first_valid):
    B, T, HD = q.shape
    KD = kpad.shape[2]
    n_heads, n_kv = HD // HEAD_DIM, KD // HEAD_DIM
    kv_specs = [pl.BlockSpec((1, CHUNK, KD), functools.partial(lambda b, c, s, o: (b, c + o, 0), o=o))
                for o in range(3)]
    return pl.pallas_call(
        functools.partial(_swa_body, first_valid=first_valid, n_heads=n_heads, n_kv=n_kv),
        grid_spec=pltpu.PrefetchScalarGridSpec(
            num_scalar_prefetch=1,
            grid=(B, T // CHUNK),
            in_specs=[pl.BlockSpec((1, CHUNK, HD), lambda b, c, s: (b, c, 0))] + kv_specs + kv_specs,
            out_specs=pl.BlockSpec((1, CHUNK, HD), lambda b, c, s: (b, c, 0))),
        out_shape=jax.ShapeDtypeStruct((B, T, HD), MXU_DTYPE),
        compiler_params=_params("arbitrary", "arbitrary"),
        name="swa_attention",
    )(sinks.astype(F32), q, kpad, kpad, kpad, vpad, vpad, vpad)


def _gmm1_body(be_ref, nu_ref, x_ref, wg_ref, wu_ref, bg_ref, bu_ref, o_ref, wg_s, wu_s):
    m = pl.program_id(1)
    changed = (m == 0) | (be_ref[m] != be_ref[jnp.maximum(m - 1, 0)])

    @pl.when(changed)
    def _():
        wg_s[...] = wg_ref[0, 0].astype(MXU_DTYPE)
        wu_s[...] = wu_ref[0, 0].astype(MXU_DTYPE)

    @pl.when(m < nu_ref[0])
    def _():
        x = x_ref[...]
        x = _unpack_pairs(x) if x.dtype == jnp.int32 else x.astype(MXU_DTYPE)
        g = jnp.dot(x, wg_s[...], preferred_element_type=F32) + bg_ref[0, 0]
        u = jnp.dot(x, wu_s[...], preferred_element_type=F32) + bu_ref[0, 0]
        g = jnp.minimum(g, SWIGLU_LIMIT)
        u = jnp.clip(u, -SWIGLU_LIMIT, SWIGLU_LIMIT)
        o_ref[...] = (g * _sigmoid(SWIGLU_ALPHA * g) * (u + 1.0)).astype(o_ref.dtype)

    @pl.when(m >= nu_ref[0])
    def _():
        o_ref[...] = jnp.zeros(o_ref.shape, o_ref.dtype)


def _gmm2_body(be_ref, nu_ref, a_ref, w_ref, b_ref, o_ref, w_s):
    m = pl.program_id(1)
    changed = (m == 0) | (be_ref[m] != be_ref[jnp.maximum(m - 1, 0)])

    @pl.when(changed)
    def _():
        w_s[...] = w_ref[0, 0].astype(MXU_DTYPE)

    @pl.when(m < nu_ref[0])
    def _():
        o_ref[...] = jnp.dot(a_ref[...], w_s[...], preferred_element_type=F32) + b_ref[0, 0]

    @pl.when(m >= nu_ref[0])
    def _():
        o_ref[...] = jnp.zeros(o_ref.shape, o_ref.dtype)


def _moe_experts(xs, blk_e, n_used, layer, w_gu, b_gu, w_dn, b_dn, *, tm, tf=512, tn=512):
    R, xw = xs.shape
    L, E, D, F2 = w_gu.shape
    Fh = F2 // 2
    tf = _tile(Fh, tf)
    tn = _tile(D, tn)
    nb = R // tm
    nf = Fh // tf

    def xmap(j, m, be, nu):
        return (jnp.minimum(m, nu[0] - 1), 0)

    act = pl.pallas_call(
        _gmm1_body,
        grid_spec=pltpu.PrefetchScalarGridSpec(
            num_scalar_prefetch=2,
            grid=(nf, nb),
            in_specs=[pl.BlockSpec((tm, xw), xmap),
                      pl.BlockSpec((1, 1, D, tf), lambda j, m, be, nu: (layer, be[m], 0, j)),
                      pl.BlockSpec((1, 1, D, tf), lambda j, m, be, nu: (layer, be[m], 0, nf + j)),
                      pl.BlockSpec((1, 1, 1, tf), lambda j, m, be, nu: (layer, be[m], 0, j)),
                      pl.BlockSpec((1, 1, 1, tf), lambda j, m, be, nu: (layer, be[m], 0, nf + j))],
            out_specs=pl.BlockSpec((tm, tf), lambda j, m, be, nu: (m, j)),
            scratch_shapes=[pltpu.VMEM((D, tf), MXU_DTYPE), pltpu.VMEM((D, tf), MXU_DTYPE)]),
        out_shape=jax.ShapeDtypeStruct((R, Fh), MXU_DTYPE),
        compiler_params=_params("arbitrary", "arbitrary"),
        name="moe_gate_up",
    )(blk_e, n_used, xs, w_gu, w_gu, b_gu.reshape(L, E, 1, F2), b_gu.reshape(L, E, 1, F2))

    ys = pl.pallas_call(
        _gmm2_body,
        grid_spec=pltpu.PrefetchScalarGridSpec(
            num_scalar_prefetch=2,
            grid=(D // tn, nb),
            in_specs=[pl.BlockSpec((tm, Fh), xmap),
                      pl.BlockSpec((1, 1, Fh, tn), lambda j, m, be, nu: (layer, be[m], 0, j)),
                      pl.BlockSpec((1, 1, 1, tn), lambda j, m, be, nu: (layer, be[m], 0, j))],
            out_specs=pl.BlockSpec((tm, tn), lambda j, m, be, nu: (m, j)),
            scratch_shapes=[pltpu.VMEM((Fh, tn), MXU_DTYPE)]),
        out_shape=jax.ShapeDtypeStruct((R, D), F32),
        compiler_params=_params("arbitrary", "arbitrary"),
        name="moe_down",
    )(blk_e, n_used, act, w_dn, b_dn.reshape(L, E, 1, D))
    return ys


def _combine_body(y_ref, gate_ref, res_ref, mg_ref, o_ref):
    gate = gate_ref[...]
    y = None
    for k in range(y_ref.shape[0]):
        t = gate[:, k:k + 1] * y_ref[k]
        y = t if y is None else y + t
    tm, d = y.shape
    o_ref[...] = res_ref[...] + mg_ref[...] * y.reshape(tm // CHUNK, CHUNK, d)


def _moe_combine(yk, gate, res3, mod_exp, k_gate, *, tm=256):
    K, N, D = yk.shape
    NB, C, _ = res3.shape
    tm = _tile(N, tm, C)
    return pl.pallas_call(
        _combine_body,
        grid=(N // tm,),
        in_specs=[pl.BlockSpec((K, tm, D), lambda i: (0, i, 0)),
                  pl.BlockSpec((tm, K), lambda i: (i, 0)),
                  pl.BlockSpec((tm // C, C, D), lambda i: (i, 0, 0)),
                  pl.BlockSpec((tm // C, 1, D), lambda i: (i, 0, k_gate))],
        out_specs=pl.BlockSpec((tm // C, C, D), lambda i: (i, 0, 0)),
        out_shape=jax.ShapeDtypeStruct((NB, C, D), F32),
        compiler_params=_params("arbitrary"),
        name="moe_combine",
    )(yk, gate, res3, mod_exp)


def _moe(h, hp, layer, res3, mod_exp, w_router, b_router, w_gu, b_gu, w_dn, b_dn, *, tm):
    N, D = h.shape
    E = w_router.shape[-1]
    wr = jnp.pad(w_router[layer], ((0, 0), (0, LANES - E)))
    br = jnp.pad(b_router[layer], (0, LANES - E)).reshape(1, LANES)
    logits = _mm(h, wr, [F32], tn=LANES, bias=br)[0][:, :E]
    top_logit, top_e = lax.top_k(logits, TOP_K)
    gate = jax.nn.softmax(top_logit, axis=-1)
    e_flat = top_e.reshape(-1)
    onehot = (e_flat[:, None] == jnp.arange(E, dtype=jnp.int32)[None, :]).astype(jnp.int32)
    rank = jnp.take_along_axis(jnp.cumsum(onehot, axis=0) - onehot, e_flat[:, None], axis=1)[:, 0]
    counts = jnp.sum(onehot, axis=0)
    padded = (counts + tm - 1) // tm * tm
    pend = jnp.cumsum(padded)
    dest = (pend - padded)[e_flat] + rank
    n_blocks = -(-N * TOP_K // tm) + E
    src_tok = jnp.zeros((n_blocks * tm,), jnp.int32).at[dest].set(jnp.arange(N * TOP_K, dtype=jnp.int32) // TOP_K)
    blk_start = jnp.arange(n_blocks, dtype=jnp.int32) * tm
    blk_e = jnp.minimum(jnp.sum((pend[None, :] <= blk_start[:, None]).astype(jnp.int32), axis=1), E - 1)
    n_used = (pend[-1] // tm).astype(jnp.int32).reshape(1)
    xs = hp[src_tok]
    ys = _moe_experts(xs, blk_e, n_used, layer, w_gu, b_gu, w_dn, b_dn, tm=tm)
    yk = ys[dest.reshape(N, TOP_K).T.reshape(-1)].reshape(TOP_K, N, D)
    return _moe_combine(yk, gate, res3, mod_exp, 5)


def kernel(x_prompt, x_sample, c_prompt, c_sample, cache_fox_kv, cache_fox_logf, cache_dsa_kv, cache_dsa_kidx, cache_swa_kv, w_ada, b_ada, g_mix, g_ffn, w_in_ab, b_forget, w_out_ab, w_in_c, sinks_c, w_out_c, w_router, b_router, w_gu, b_gu, w_dn, b_dn, g_final):
    Bp, Tp, D = x_prompt.shape
    Bs, Ts, _ = x_sample.shape
    depth = w_ada.shape[0]
    past_len = cache_fox_kv.shape[2]
    h_fox = cache_fox_kv.shape[4]
    kv_dsa = cache_dsa_kv.shape[4]
    d_idx = cache_dsa_kidx.shape[3]
    window = cache_swa_kv.shape[2]
    kv_swa = cache_swa_kv.shape[4]
    h_swa = sinks_c.shape[1]
    h_dsa = w_out_ab.shape[1] // HEAD_DIM - h_fox
    fox_w, dq_w, dkv_w = h_fox * HEAD_DIM, h_dsa * HEAD_DIM, kv_dsa * HEAD_DIM
    h_idx = (w_in_ab.shape[2] - 3 * fox_w - h_fox - dq_w - 2 * dkv_w - d_idx) // (d_idx + 1)
    assert Bp == 1 and Ts == CHUNK and Tp % CHUNK == 0 and past_len % CHUNK == 0
    assert d_idx == HEAD_DIM and window == 2 * CHUNK and d_idx + h_fox + h_idx <= LANES
    Mp, Ms = Bp * Tp, Bs * Ts
    M = Mp + Ms
    NB = M // CHUNK
    hd_scale = HEAD_DIM ** -0.5

    pos = jnp.concatenate([jnp.arange(Tp, dtype=jnp.int32),
                           jnp.tile(past_len + jnp.arange(Ts, dtype=jnp.int32), Bs)])
    half = HEAD_DIM // 2
    inv = ROPE_THETA ** (-jnp.arange(half, dtype=F32) / half)
    ang = pos.astype(F32)[:, None] * inv
    cos, sin = jnp.cos(ang), jnp.sin(ang)
    rope = (jnp.concatenate([cos, cos, cos, cos], axis=1), jnp.concatenate([-sin, sin, -sin, sin], axis=1))

    n_c = Bp + Bs
    c_all = jnp.pad(jnp.concatenate([c_prompt, c_sample], axis=0), ((0, -n_c % 8), (0, 0)))
    mod = _ada(c_all, w_ada, b_ada)
    blk_row = jnp.concatenate([jnp.zeros((Mp // CHUNK,), jnp.int32),
                               Bp + jnp.arange(Bs, dtype=jnp.int32)])
    x3 = jnp.concatenate([x_prompt.reshape(Mp // CHUNK, CHUNK, D), x_sample.reshape(Ms // CHUNK, CHUNK, D)], axis=0)

    fox_kv_p, fox_kv_s, logf_p, logf_s, dsa_kv_p, dsa_kv_s, kidx_p, kidx_s, swa_p, swa_s = ([] for _ in range(10))
    moe_tm = 256 if M * TOP_K >= 8192 else 64

    for l in range(depth):
        mod_exp = mod[l][blk_row][:, None, :]
        h = _norm_mod(x3, g_mix[l], mod_exp, 0, 1).reshape(M, D)
        j = l // 2
        if l % 2 == 0:
            w = w_in_ab[j]
            offs = np.cumsum([0, fox_w, fox_w, fox_w, h_fox, dq_w, dkv_w, dkv_w, h_idx * d_idx, d_idx, h_idx])
            seg = lambda a, b: w[:, offs[a]:offs[b]].astype(MXU_DTYPE)
            q_fox = _mm(h, seg(0, 1), [MXU_DTYPE], scale=hd_scale)[0]
            fkv32, fkv = _mm(h, seg(1, 3), [F32, MXU_DTYPE])
            q_dsa = _mm(h, seg(4, 5), [MXU_DTYPE], scale=hd_scale, rope=rope, rope_tiles=10 ** 6)[0]
            dkv32 = _mm(h, seg(5, 7), [F32], tn=dkv_w, rope=rope, rope_tiles=1)[0]
            qi = _mm(h, seg(7, 8), [MXU_DTYPE], scale=d_idx ** -0.5, rope=rope, rope_tiles=10 ** 6)[0]
            pad = LANES - d_idx - h_fox - h_idx
            w_small = jnp.concatenate([w[:, offs[8]:offs[9]], w[:, offs[3]:offs[4]], w[:, offs[9]:offs[10]],
                                       jnp.zeros((D, pad), w.dtype)], axis=1).astype(MXU_DTYPE)
            bf_row = jnp.pad(b_forget[j], (d_idx, LANES - d_idx - h_fox)).reshape(1, LANES)
            small = _mm_small(h, w_small, rope, bf_row, d_idx=d_idx, h_fox=h_fox, wi_scale=h_idx ** -0.5)
            kidx, logf, wi = small[:, :d_idx], small[:, d_idx:d_idx + h_fox], small[:, d_idx + h_fox:d_idx + h_fox + h_idx]

            fox_kv_p.append(fkv32[:Mp].reshape(Bp, Tp, 2, h_fox, HEAD_DIM))
            fox_kv_s.append(fkv32[Mp:].reshape(Bs, Ts, 2, h_fox, HEAD_DIM))
            logf_p.append(logf[:Mp].reshape(Bp, Tp, h_fox))
            logf_s.append(logf[Mp:].reshape(Bs, Ts, h_fox))
            dsa_kv_p.append(dkv32[:Mp].reshape(Bp, Tp, 2, kv_dsa, HEAD_DIM))
            dsa_kv_s.append(dkv32[Mp:].reshape(Bs, Ts, 2, kv_dsa, HEAD_DIM))
            kidx_p.append(kidx[:Mp].reshape(Bp, Tp, d_idx))
            kidx_s.append(kidx[Mp:].reshape(Bs, Ts, d_idx))

            cum_p = jnp.cumsum(logf_p[-1], axis=1)
            oa_p = _fox_attention(q_fox[:Mp].reshape(Bp, Tp, fox_w), fkv[:Mp].reshape(Bp, Tp, 2 * fox_w),
                                  cum_p, cum_p, tq=min(256, Tp), tk=min(256, Tp))
            kv_s = jnp.concatenate([cache_fox_kv[j].reshape(Bs, past_len, 2 * fox_w).astype(MXU_DTYPE),
                                    fkv[Mp:].reshape(Bs, Ts, 2 * fox_w)], axis=1)
            cum_s = jnp.cumsum(jnp.concatenate([cache_fox_logf[j].astype(F32), logf_s[-1]], axis=1), axis=1)
            oa_s = _fox_attention(q_fox[Mp:].reshape(Bs, Ts, fox_w), kv_s, cum_s[:, past_len:], cum_s,
                                  tq=Ts, tk=_tile(past_len, 256, CHUNK))
            dkv = dkv32.astype(MXU_DTYPE)
            kid = jnp.concatenate([kidx, kidx], axis=1).astype(MXU_DTYPE)
            ob_p = _dsa_attention(q_dsa[:Mp].reshape(Bp, Tp, dq_w), dkv[:Mp, :dkv_w].reshape(Bp, Tp, dkv_w),
                                  dkv[:Mp, dkv_w:].reshape(Bp, Tp, dkv_w), qi[:Mp].reshape(Bp, Tp, -1),
                                  kid[:Mp].reshape(Bp, Tp, LANES), wi[:Mp].reshape(Bp, Tp, h_idx),
                                  s_real=Tp, topk=min(TOPK_MAX, Tp // 4), tq=min(128, Tp), tk=min(512, Tp))
            S = past_len + Ts
            s_pad = -(-S // LANES) * LANES
            padk = lambda a: jnp.pad(a, ((0, 0), (0, s_pad - S), (0, 0)))
            pkv = cache_dsa_kv[j].reshape(Bs, past_len, 2 * dkv_w).astype(MXU_DTYPE)
            k_s = padk(jnp.concatenate([pkv[:, :, :dkv_w], dkv[Mp:, :dkv_w].reshape(Bs, Ts, dkv_w)], axis=1))
            v_s = padk(jnp.concatenate([pkv[:, :, dkv_w:], dkv[Mp:, dkv_w:].reshape(Bs, Ts, dkv_w)], axis=1))
            pki = cache_dsa_kidx[j].astype(MXU_DTYPE)
            kid_s = padk(jnp.concatenate([jnp.concatenate([pki, pki], axis=2), kid[Mp:].reshape(Bs, Ts, LANES)], axis=1))
            ob_s = _dsa_attention(q_dsa[Mp:].reshape(Bs, Ts, dq_w), k_s, v_s, qi[Mp:].reshape(Bs, Ts, -1),
                                  kid_s, wi[Mp:].reshape(Bs, Ts, h_idx),
                                  s_real=S, topk=min(TOPK_MAX, S // 4), tq=Ts, tk=LANES)
            oa = jnp.concatenate([oa_p.reshape(Mp, fox_w), oa_s.reshape(Ms, fox_w)], axis=0)
            ob = jnp.concatenate([ob_p.reshape(Mp, dq_w), ob_s.reshape(Ms, dq_w)], axis=0)
            if fox_w == dq_w:
                x3 = _mm_res([oa, ob], w_out_ab, j, x3, mod_exp, 2)
            else:
                x3 = _mm_res([jnp.concatenate([oa, ob], axis=1)], w_out_ab, j, x3, mod_exp, 2)
        else:
            w = w_in_c[j]
            qw, kw = h_swa * HEAD_DIM, kv_swa * HEAD_DIM
            q = _mm(h, w[:, :qw].astype(MXU_DTYPE), [MXU_DTYPE], scale=hd_scale, rope=rope, rope_tiles=10 ** 6)[0]
            kv32 = _mm(h, w[:, qw:].astype(MXU_DTYPE), [F32], tn=kw, rope=rope, rope_tiles=1)[0]
            kvb = kv32.astype(MXU_DTYPE)
            kv_new_p = kv32[:Mp].reshape(Bp, Tp, 2, kv_swa, HEAD_DIM)
            kv_new_s = kv32[Mp:].reshape(Bs, Ts, 2, kv_swa, HEAD_DIM)
            swa_p.append(kv_new_p[:, Tp - min(window, Tp):])
            swa_s.append(jnp.concatenate([cache_swa_kv[j], kv_new_s], axis=1)[:, Ts:])
            zpad = jnp.zeros((Bp, window, kw), MXU_DTYPE)
            o_p = _swa_attention(q[:Mp].reshape(Bp, Tp, qw),
                                 jnp.concatenate([zpad, kvb[:Mp, :kw].reshape(Bp, Tp, kw)], axis=1),
                                 jnp.concatenate([zpad, kvb[:Mp, kw:].reshape(Bp, Tp, kw)], axis=1),
                                 sinks_c[j], first_valid=window)
            past = cache_swa_kv[j].reshape(Bs, window, 2 * kw).astype(MXU_DTYPE)
            o_s = _swa_attention(q[Mp:].reshape(Bs, Ts, qw),
                                 jnp.concatenate([past[:, :, :kw], kvb[Mp:, :kw].reshape(Bs, Ts, kw)], axis=1),
                                 jnp.concatenate([past[:, :, kw:], kvb[Mp:, kw:].reshape(Bs, Ts, kw)], axis=1),
                                 sinks_c[j], first_valid=0)
            o = jnp.concatenate([o_p.reshape(Mp, qw), o_s.reshape(Ms, qw)], axis=0)
            x3 = _mm_res([o], w_out_c, j, x3, mod_exp, 2)

        h2, h2p = _norm_mod(x3, g_ffn[l], mod_exp, 3, 4, packed=True)
        x3 = _moe(h2.reshape(M, D), h2p.reshape(M, -1), l, x3, mod_exp,
                  w_router, b_router, w_gu, b_gu, w_dn, b_dn, tm=moe_tm)

    y = _final_norm(x3, g_final).reshape(M, D)
    st = lambda xs: jnp.stack(xs, axis=0)
    return (y[:Mp].reshape(Bp, Tp, D), y[Mp:].reshape(Bs, Ts, D),
            st(fox_kv_p), st(fox_kv_s), st(logf_p), st(logf_s), st(dsa_kv_p), st(dsa_kv_s),
            st(kidx_p), st(kidx_s), st(swa_p), st(swa_s))
```

```python
import functools

import numpy as np
import jax
import jax.numpy as jnp
from jax import lax
from jax.experimental import pallas as pl
from jax.experimental.pallas import tpu as pltpu

CHUNK = 64
HEAD_DIM = 64
ROPE_THETA = 10000.0
EPS = 1e-5
NEG = -1e30
TOPK_MAX = 256
TOP_K = 4
SWIGLU_ALPHA = 1.702
SWIGLU_LIMIT = 7.0

LANES = 128
F32 = jnp.float32
MXU_DTYPE = jnp.bfloat16
VMEM_LIMIT = 56 * 1024 * 1024

INT_MIN = -2 ** 31
_negbits = int(np.array(NEG, np.float32).view(np.int32))
KEY_NEG = _negbits ^ 0x7FFFFFFF

_NT = (((1,), (1,)), ((), ()))


def _params(*sem):
    return pltpu.CompilerParams(dimension_semantics=sem, vmem_limit_bytes=VMEM_LIMIT)


def _tile(n, pref, mult=LANES):
    if n <= pref:
        return n
    t = pref - pref % mult
    while t >= mult:
        if n % t == 0:
            return t
        t -= mult
    return n


def _sigmoid(x):
    return 1.0 / (1.0 + jnp.exp(-x))


def _ada_body(c_ref, w_ref, b_ref, o_ref):
    c = c_ref[...]
    a = (c * _sigmoid(c)).astype(MXU_DTYPE)
    o_ref[0] = jnp.dot(a, w_ref[0].astype(MXU_DTYPE), preferred_element_type=F32) + b_ref[0]


def _ada(c_all, w_ada, b_ada):
    L, D, N = w_ada.shape
    R = c_all.shape[0]
    tn = _tile(N, 1024)
    return pl.pallas_call(
        _ada_body,
        grid=(L, N // tn),
        in_specs=[pl.BlockSpec((R, D), lambda l, j: (0, 0)),
                  pl.BlockSpec((1, D, tn), lambda l, j: (l, 0, j)),
                  pl.BlockSpec((1, 1, tn), lambda l, j: (l, 0, j))],
        out_specs=pl.BlockSpec((1, R, tn), lambda l, j: (l, 0, j)),
        out_shape=jax.ShapeDtypeStruct((L, R, N), F32),
        compiler_params=_params("arbitrary", "arbitrary"),
        name="ada_mod",
    )(c_all, w_ada, b_ada.reshape(L, 1, N))


def _norm_mod_body(x_ref, g_ref, sh_ref, sc_ref, o_ref):
    x = x_ref[...]
    ms = jnp.mean(x * x, axis=-1, keepdims=True)
    xn = x * lax.rsqrt(ms + EPS) * g_ref[...]
    o_ref[...] = (xn * (1.0 + sc_ref[...]) + sh_ref[...]).astype(o_ref.dtype)


def _norm_body(x_ref, g_ref, o_ref):
    x = x_ref[...]
    ms = jnp.mean(x * x, axis=-1, keepdims=True)
    o_ref[...] = (x * lax.rsqrt(ms + EPS) * g_ref[...]).astype(o_ref.dtype)


def _pack_pairs(h):
    half = h.shape[-1] // 2
    bits = pltpu.bitcast(h.astype(jnp.bfloat16).astype(F32), jnp.int32)
    return bits[..., half:] | lax.shift_right_logical(bits[..., :half], 16)


def _unpack_pairs(w):
    lo = pltpu.bitcast(lax.shift_left(w, 16), F32).astype(jnp.bfloat16)
    hi = pltpu.bitcast(w & jnp.int32(-65536), F32).astype(jnp.bfloat16)
    return jnp.concatenate([lo, hi], axis=1)


def _norm_mod_pack_body(x_ref, g_ref, sh_ref, sc_ref, o_ref, p_ref):
    x = x_ref[...]
    ms = jnp.mean(x * x, axis=-1, keepdims=True)
    xn = x * lax.rsqrt(ms + EPS) * g_ref[...]
    h = xn * (1.0 + sc_ref[...]) + sh_ref[...]
    o_ref[...] = h.astype(o_ref.dtype)
    p_ref[...] = _pack_pairs(h) if p_ref.dtype == jnp.int32 else h


def _norm_mod(x3, g, mod_exp, k_shift, k_scale, packed=False):
    NB, C, D = x3.shape
    bb = _tile(NB, 8, 1)
    blk = pl.BlockSpec((bb, C, D), lambda i: (i, 0, 0))
    out_specs, out_shape, body = blk, jax.ShapeDtypeStruct((NB, C, D), MXU_DTYPE), _norm_mod_body
    if packed:
        body = _norm_mod_pack_body
        if MXU_DTYPE == jnp.bfloat16:
            out_specs = [blk, pl.BlockSpec((bb, C, D // 2), lambda i: (i, 0, 0))]
            out_shape = [out_shape, jax.ShapeDtypeStruct((NB, C, D // 2), jnp.int32)]
        else:
            out_specs = [blk, blk]
            out_shape = [out_shape, jax.ShapeDtypeStruct((NB, C, D), F32)]
    return pl.pallas_call(
        body,
        grid=(NB // bb,),
        in_specs=[blk,
                  pl.BlockSpec((1, 1, D), lambda i: (0, 0, 0)),
                  pl.BlockSpec((bb, 1, D), lambda i: (i, 0, k_shift)),
                  pl.BlockSpec((bb, 1, D), lambda i: (i, 0, k_scale))],
        out_specs=out_specs,
        out_shape=out_shape,
        compiler_params=_params("arbitrary"),
        name="norm_mod",
    )(x3, g.reshape(1, 1, D), mod_exp, mod_exp)


def _final_norm(x3, g):
    NB, C, D = x3.shape
    bb = _tile(NB, 8, 1)
    return pl.pallas_call(
        _norm_body,
        grid=(NB // bb,),
        in_specs=[pl.BlockSpec((bb, C, D), lambda i: (i, 0, 0)),
                  pl.BlockSpec((1, 1, D), lambda i: (0, 0, 0))],
        out_specs=pl.BlockSpec((bb, C, D), lambda i: (i, 0, 0)),
        out_shape=jax.ShapeDtypeStruct((NB, C, D), F32),
        compiler_params=_params("arbitrary"),
        name="final_norm",
    )(x3, g.reshape(1, 1, D))


def _rope_tile(acc, cos, sin):
    tn = acc.shape[1]
    reps = tn // LANES
    cosf = jnp.concatenate([cos] * reps, axis=1) if reps > 1 else cos
    sinf = jnp.concatenate([sin] * reps, axis=1) if reps > 1 else sin
    lane = lax.broadcasted_iota(jnp.int32, acc.shape, 1)
    first_half = (lane & (HEAD_DIM - 1)) < HEAD_DIM // 2
    rot = jnp.where(first_half, pltpu.roll(acc, tn - HEAD_DIM // 2, 1), pltpu.roll(acc, HEAD_DIM // 2, 1))
    return acc * cosf + rot * sinf


def _mm_body(*refs, scale, rope_tiles, n_col_tiles, has_rope, has_bias):
    x_ref, w_ref = refs[:2]
    idx = 2
    if has_rope:
        cos_ref, sin_ref = refs[idx:idx + 2]
        idx += 2
    if has_bias:
        b_ref = refs[idx]
        idx += 1
    out_refs = refs[idx:]
    acc = jnp.dot(x_ref[...], w_ref[...].astype(MXU_DTYPE), preferred_element_type=F32)
    if has_bias:
        acc = acc + b_ref[...]

    def emit(val):
        if scale != 1.0:
            val = val * scale
        for o in out_refs:
            o[...] = val.astype(o.dtype)

    if has_rope and rope_tiles > 0:
        if rope_tiles >= n_col_tiles:
            emit(_rope_tile(acc, cos_ref[...], sin_ref[...]))
        else:
            j = pl.program_id(1)

            @pl.when(j < rope_tiles)
            def _():
                emit(_rope_tile(acc, cos_ref[...], sin_ref[...]))

            @pl.when(j >= rope_tiles)
            def _():
                emit(acc)
    else:
        emit(acc)


def _mm(x, w, out_dtypes, *, tm=1024, tn=512, scale=1.0, rope=None, rope_tiles=0, bias=None):
    M, K = x.shape
    N = w.shape[1]
    tm = _tile(M, tm, 64)
    tn = _tile(N, tn)
    n_col = N // tn
    in_specs = [pl.BlockSpec((tm, K), lambda i, j: (i, 0)),
                pl.BlockSpec((K, tn), lambda i, j: (0, j))]
    args = [x, w]
    if rope is not None:
        in_specs += [pl.BlockSpec((tm, LANES), lambda i, j: (i, 0))] * 2
        args += list(rope)
    if bias is not None:
        in_specs.append(pl.BlockSpec((1, tn), lambda i, j: (0, j)))
        args.append(bias)
    outs = pl.pallas_call(
        functools.partial(_mm_body, scale=scale, rope_tiles=rope_tiles if rope is not None else 0,
                          n_col_tiles=n_col, has_rope=rope is not None, has_bias=bias is not None),
        grid=(M // tm, n_col),
        in_specs=in_specs,
        out_specs=[pl.BlockSpec((tm, tn), lambda i, j: (i, j)) for _ in out_dtypes],
        out_shape=[jax.ShapeDtypeStruct((M, N), dt) for dt in out_dtypes],
        compiler_params=_params("arbitrary", "arbitrary"),
        name="proj",
    )(*args)
    return outs


def _small_body(x_ref, w_ref, cos_ref, sin_ref, bf_ref, o_ref, *, d_idx, h_fox, wi_scale):
    acc = jnp.dot(x_ref[...], w_ref[...].astype(MXU_DTYPE), preferred_element_type=F32)
    roped = _rope_tile(acc, cos_ref[...], sin_ref[...])
    z = acc + bf_ref[...]
    logf = jnp.minimum(z, 0.0) - jnp.log1p(jnp.exp(-jnp.abs(z)))
    lane = lax.broadcasted_iota(jnp.int32, acc.shape, 1)
    o_ref[...] = jnp.where(lane < d_idx, roped, jnp.where(lane < d_idx + h_fox, logf, acc * wi_scale))


def _mm_small(x, w, rope, bf_row, *, d_idx, h_fox, wi_scale, tm=1024):
    M, K = x.shape
    tm = _tile(M, tm, 64)
    return pl.pallas_call(
        functools.partial(_small_body, d_idx=d_idx, h_fox=h_fox, wi_scale=wi_scale),
        grid=(M // tm,),
        in_specs=[pl.BlockSpec((tm, K), lambda i: (i, 0)),
                  pl.BlockSpec((K, LANES), lambda i: (0, 0)),
                  pl.BlockSpec((tm, LANES), lambda i: (i, 0)),
                  pl.BlockSpec((tm, LANES), lambda i: (i, 0)),
                  pl.BlockSpec((1, LANES), lambda i: (0, 0))],
        out_specs=pl.BlockSpec((tm, LANES), lambda i: (i, 0)),
        out_shape=jax.ShapeDtypeStruct((M, LANES), F32),
        compiler_params=_params("arbitrary"),
        name="proj_small",
    )(x, w, rope[0], rope[1], bf_row)


def _mm_res_body(*refs, n_pairs):
    xs = refs[:n_pairs]
    ws = refs[n_pairs:2 * n_pairs]
    res_ref, gate_ref, o_ref = refs[2 * n_pairs:]
    acc = None
    for x_ref, w_ref in zip(xs, ws):
        part = jnp.dot(x_ref[...], w_ref[0].astype(MXU_DTYPE), preferred_element_type=F32)
        acc = part if acc is None else acc + part
    tm, tn = acc.shape
    o_ref[...] = res_ref[...] + gate_ref[...] * acc.reshape(tm // CHUNK, CHUNK, tn)


def _mm_res(xs, w_all, layer, res3, mod_exp, k_gate, *, tm=1024, tn=512):
    NB, C, N = res3.shape
    M = NB * C
    tm = _tile(M, tm, C)
    tn = _tile(N, tn)
    n_col = N // tn
    kp = xs[0].shape[1]
    assert all(x.shape == (M, kp) for x in xs) and w_all.shape[1] == kp * len(xs)
    in_specs = [pl.BlockSpec((tm, kp), lambda i, j: (i, 0)) for _ in xs]
    in_specs += [pl.BlockSpec((1, kp, tn), functools.partial(lambda i, j, p: (layer, p, j), p=p))
                 for p in range(len(xs))]
    in_specs += [pl.BlockSpec((tm // C, C, tn), lambda i, j: (i, 0, j)),
                 pl.BlockSpec((tm // C, 1, tn), lambda i, j: (i, 0, k_gate * n_col + j))]
    return pl.pallas_call(
        functools.partial(_mm_res_body, n_pairs=len(xs)),
        grid=(M // tm, n_col),
        in_specs=in_specs,
        out_specs=pl.BlockSpec((tm // C, C, tn), lambda i, j: (i, 0, j)),
        out_shape=jax.ShapeDtypeStruct((NB, C, N), F32),
        compiler_params=_params("arbitrary", "arbitrary"),
        name="out_proj_res",
    )(*xs, *([w_all] * len(xs)), res3, mod_exp)


def _softmax_steps(sts, vt, carry):
    stats, ps = [], []
    for h, st in enumerate(sts):
        m, l = carry[3 * h], carry[3 * h + 1]
        m_new = jnp.maximum(m, jnp.max(st, axis=0, keepdims=True))
        alpha = jnp.exp(m - m_new)
        p = jnp.exp(st - m_new)
        stats.append((m_new, alpha * l + jnp.sum(p, axis=0, keepdims=True), alpha))
        ps.append(p.astype(MXU_DTYPE))
    new = []
    for h, p in enumerate(ps):
        m_new, l, alpha = stats[h]
        new += [m_new, l, alpha * carry[3 * h + 2] + jnp.dot(vt, p, preferred_element_type=F32)]
    return tuple(new)


def _fox_body(q_ref, k_ref, v_ref, o_ref, *, tq, tk, q_off, nq):
    q0 = q_off if nq == 1 else q_off + pl.program_id(2) * tq
    n_full = q0 // tk

    def step(kc, vt, carry, causal):
        sts = [jnp.dot(kc, q_ref[0, h], preferred_element_type=F32) for h in range(2)]
        if causal:
            row = lax.broadcasted_iota(jnp.int32, sts[0].shape, 0)
            col = lax.broadcasted_iota(jnp.int32, sts[0].shape, 1)
            sts = [jnp.where(row <= col, st, NEG) for st in sts]
        return _softmax_steps(sts, vt, carry)

    def full_chunk(c, carry):
        kc = k_ref[0, 0, pl.ds(pl.multiple_of(c * tk, tk), tk), :]
        return step(kc, v_ref[0, 0, c], carry, False)

    init = (jnp.full((1, tq), NEG, F32), jnp.zeros((1, tq), F32), jnp.zeros((LANES, tq), F32)) * 2
    carry = lax.fori_loop(0, n_full, full_chunk, init)
    k0 = n_full * tk
    kc = k_ref[0, 0, pl.ds(k0 if isinstance(k0, int) else pl.multiple_of(k0, tk), tq), :]
    carry = step(kc, v_ref[0, 0, n_full][:, :tq], carry, True)
    o0, o1 = carry[2] / carry[1], carry[5] / carry[4]
    o_ref[0, 0] = jnp.concatenate([o0[:HEAD_DIM], o1[HEAD_DIM:]], axis=0).astype(o_ref.dtype)


def _split3(x):
    hi = x.astype(MXU_DTYPE)
    r1 = x - hi.astype(F32)
    mid = r1.astype(MXU_DTYPE)
    lo = (r1 - mid.astype(F32)).astype(MXU_DTYPE)
    return [hi, mid, lo]


def _fox_attention(q, kv, cum_q, cum_k, *, tq, tk):
    B, T, HD = q.shape
    S = kv.shape[1]
    P = HD // LANES
    q_off = S - T
    nq = T // tq
    assert T % tq == 0 and q_off % tk == 0 and (tq == tk or (nq == 1 and tq <= tk))
    nc = -(-S // tk)
    one_k = jnp.ones((B, S, P), MXU_DTYPE)
    one_q = jnp.ones((B, T, P), MXU_DTYPE)
    zero_q = jnp.zeros((B, T, P), MXU_DTYPE)
    ck3 = [c.reshape(B, S, P, 2) for c in _split3(-cum_k)]
    cq3 = [c.reshape(B, T, P, 2) for c in _split3(cum_q)]
    k_aug, q_aug = [], [[], []]
    for e in range(2):
        k_aug += [c[..., e] for c in ck3] + [one_k] * 3
        for e2 in range(2):
            q_aug[e2] += ([one_q] * 3 + [c[..., e] for c in cq3]) if e == e2 else [zero_q] * 6
    pad = LANES - len(k_aug)
    k_aug = jnp.pad(jnp.stack(k_aug, axis=-1), ((0, 0), (0, 0), (0, 0), (0, pad)))
    kk = jnp.concatenate([kv[..., :HD].reshape(B, S, P, LANES), k_aug], axis=-1).transpose(0, 2, 1, 3)
    qp = q.reshape(B, T, P, 2, HEAD_DIM)
    zq = jnp.zeros((B, T, P, HEAD_DIM), q.dtype)
    heads = []
    for e in range(2):
        qa = jnp.pad(jnp.stack(q_aug[e], axis=-1), ((0, 0), (0, 0), (0, 0), (0, pad)))
        halves = [qp[:, :, :, 0], zq] if e == 0 else [zq, qp[:, :, :, 1]]
        heads.append(jnp.concatenate(halves + [qa], axis=-1))
    qt = jnp.stack(heads, axis=3).transpose(0, 2, 3, 4, 1).reshape(B, 2 * P, 2 * LANES, T)
    v = jnp.pad(kv[..., HD:], ((0, 0), (0, nc * tk - S), (0, 0)))
    vt = v.reshape(B, nc, tk, P, LANES).transpose(0, 3, 1, 4, 2)
    ot = pl.pallas_call(
        functools.partial(_fox_body, tq=tq, tk=tk, q_off=q_off, nq=nq),
        grid=(B, P, nq),
        in_specs=[pl.BlockSpec((1, 2, 2 * LANES, tq), lambda b, p, i: (b, p, 0, i)),
                  pl.BlockSpec((1, 1, S, 2 * LANES), lambda b, p, i: (b, p, 0, 0)),
                  pl.BlockSpec((1, 1, nc, LANES, tk), lambda b, p, i: (b, p, 0, 0, 0))],
        out_specs=pl.BlockSpec((1, 1, LANES, tq), lambda b, p, i: (b, p, 0, i)),
        out_shape=jax.ShapeDtypeStruct((B, P, LANES, T), MXU_DTYPE),
        compiler_params=_params("arbitrary", "arbitrary", "arbitrary"),
        name="fox_attention",
    )(qt, kk, vt)
    return ot.transpose(0, 3, 1, 2).reshape(B, T, HD)


def _dsa_body(qt_ref, k_ref, vt_ref, qit_ref, kid_ref, wit_ref, o_ref, keys_scr, bias_scr, j_scr, *,
              tq, tk, q_off, s_real, topk, n_heads, n_kv, n_idx, nq, n_chunks):
    if nq == 1:
        q0 = q_off
        n_ch = min((q0 + tq + tk - 1) // tk, n_chunks)
        unscanned = float(s_real - min(n_ch * tk, s_real))
    else:
        q0 = q_off + pl.program_id(1) * tq
        n_ch = jnp.minimum((q0 + tq + tk - 1) // tk, n_chunks)
        unscanned = (s_real - jnp.minimum(n_ch * tk, s_real)).astype(F32)
    krow = lax.broadcasted_iota(jnp.int32, (tk, tq), 0)
    qchunk = (q0 + lax.broadcasted_iota(jnp.int32, (tk, tq), 1)) // CHUNK

    def admissible(kpos):
        return ((kpos // CHUNK) <= qchunk) & (kpos < s_real)

    wit = wit_ref[0]

    def score_chunk(c, _):
        kc = kid_ref[0, pl.ds(pl.multiple_of(c * tk, tk), tk), :]
        score = jnp.zeros((tk, tq), F32)
        for h in range(n_idx):
            s = jnp.dot(kc, qit_ref[0, h], preferred_element_type=F32)
            score = score + wit[h:h + 1, :] * jnp.maximum(s, 0.0)
        kpos = c * tk + krow
        sm = jnp.where(admissible(kpos), score, NEG)
        bits = pltpu.bitcast(sm, jnp.int32)
        key = jnp.where(bits < 0, bits ^ 0x7FFFFFFF, bits)
        key = jnp.where(bits == INT_MIN, 0, key)
        keys_scr[c] = jnp.where(kpos < s_real, key, INT_MIN)
        return 0

    lax.fori_loop(0, n_ch, score_chunk, 0)

    def count(pred):
        def body(c, acc):
            x = jnp.where(pred(keys_scr[c], c), 1.0, 0.0)
            return acc + jnp.sum(x.reshape(tk // 8, 8, tq), axis=0)
        acc = lax.fori_loop(0, n_ch, body, jnp.zeros((8, tq), F32))
        return jnp.sum(acc, axis=0, keepdims=True)

    kf = float(topk)

    def bit_step(it, carry):
        t, cge = carry
        cand = t + jnp.left_shift(jnp.int32(1), 31 - it)
        cnt = count(lambda kc, c: kc >= cand) + jnp.where(cand <= KEY_NEG, unscanned, 0.0)
        ok = cnt >= kf
        return jnp.where(ok, cand, t), jnp.where(ok, cnt, cge)

    thr, cge = lax.fori_loop(0, 32, bit_step,
                             (jnp.full((1, tq), INT_MIN, jnp.int32), jnp.full((1, tq), 3e38, F32)))
    cgt = count(lambda kc, c: kc > thr) + jnp.where(thr < KEY_NEG, unscanned, 0.0)
    need = kf - cgt

    n_bits = int(n_chunks * tk).bit_length()
    j_scr[...] = jnp.full((1, tq), 2 ** n_bits, jnp.int32)

    @pl.when(jnp.max(cge) > kf)
    def _():
        def j_step(it, jcur):
            cand = jcur + jnp.left_shift(jnp.int32(1), n_bits - 1 - it)
            f = count(lambda kc, c: (kc == thr) & ((c * tk + krow) < cand))
            return jnp.where(f < need, cand, jcur)
        j_scr[...] = lax.fori_loop(0, n_bits, j_step, jnp.zeros((1, tq), jnp.int32))

    jmax = j_scr[...]

    def bias_chunk(c, _):
        kc = keys_scr[c]
        kpos = c * tk + krow
        sel = (kc > thr) | ((kc == thr) & (kpos <= jmax))
        bias_scr[c] = jnp.where(sel & admissible(kpos), 0.0, NEG)
        return 0

    lax.fori_loop(0, n_ch, bias_chunk, 0)

    G = n_heads // n_kv
    outs = []
    for g in range(n_kv):
        r, kh = g // 2, g % 2

        def attend(c, carry, g=g, r=r):
            kc = k_ref[0, pl.ds(pl.multiple_of(c * tk, tk), tk), r * LANES:(r + 1) * LANES]
            vt = vt_ref[0, r, c]
            b = bias_scr[c]
            sts = [jnp.dot(kc, qt_ref[0, g * G + jh], preferred_element_type=F32) + b for jh in range(G)]
            return _softmax_steps(sts, vt, carry)

        init = (jnp.full((1, tq), NEG, F32), jnp.zeros((1, tq), F32), jnp.zeros((LANES, tq), F32)) * G
        carry = lax.fori_loop(0, n_ch, attend, init)
        for jh in range(G):
            outs.append((carry[3 * jh + 2] / carry[3 * jh + 1])[kh * HEAD_DIM:(kh + 1) * HEAD_DIM])
    o_ref[0] = jnp.concatenate(outs, axis=0).astype(o_ref.dtype)


def _dsa_attention(q, k, v, qi, kid, wi, *, s_real, topk, tq, tk):
    B, T, HD = q.shape
    s_pad = k.shape[1]
    n_heads, n_kv, n_idx = HD // HEAD_DIM, k.shape[2] // HEAD_DIM, qi.shape[2] // HEAD_DIM
    assert n_kv % 2 == 0 and n_heads % n_kv == 0 and s_pad % tk == 0 and T % tq == 0
    G = n_heads // n_kv
    nq = T // tq
    nc = s_pad // tk
    qh = q.reshape(B, T, n_heads, HEAD_DIM)
    upper = ((np.arange(n_heads) // G) % 2 == 1)[None, None, :, None]
    zq = jnp.zeros_like(qh)
    qt = jnp.concatenate([jnp.where(upper, zq, qh), jnp.where(upper, qh, zq)], axis=-1).transpose(0, 2, 3, 1)
    qih = qi.reshape(B, T, n_idx, HEAD_DIM)
    qit = jnp.concatenate([qih, jnp.zeros_like(qih)], axis=-1).transpose(0, 2, 3, 1)
    wit = wi.transpose(0, 2, 1)
    vt = v.reshape(B, nc, tk, n_kv // 2, LANES).transpose(0, 3, 1, 4, 2)
    body = functools.partial(_dsa_body, tq=tq, tk=tk, q_off=s_real - T, s_real=s_real, topk=topk,
                             n_heads=n_heads, n_kv=n_kv, n_idx=n_idx, nq=nq, n_chunks=nc)
    ot = pl.pallas_call(
        body,
        grid=(B, nq),
        in_specs=[pl.BlockSpec((1, n_heads, LANES, tq), lambda b, i: (b, 0, 0, i)),
                  pl.BlockSpec((1, s_pad, k.shape[2]), lambda b, i: (b, 0, 0)),
                  pl.BlockSpec((1, n_kv // 2, nc, LANES, tk), lambda b, i: (b, 0, 0, 0, 0)),
                  pl.BlockSpec((1, n_idx, LANES, tq), lambda b, i: (b, 0, 0, i)),
                  pl.BlockSpec((1, s_pad, LANES), lambda b, i: (b, 0, 0)),
                  pl.BlockSpec((1, n_idx, tq), lambda b, i: (b, 0, i))],
        out_specs=pl.BlockSpec((1, HD, tq), lambda b, i: (b, 0, i)),
        out_shape=jax.ShapeDtypeStruct((B, HD, T), MXU_DTYPE),
        scratch_shapes=[pltpu.VMEM((nc, tk, tq), jnp.int32),
                        pltpu.VMEM((nc, tk, tq), F32),
                        pltpu.VMEM((1, tq), jnp.int32)],
        compiler_params=_params("arbitrary", "arbitrary"),
        name="dsa_attention",
    )(qt, k, vt, qit, kid, wit)
    return ot.transpose(0, 2, 1)


def _swa_body(sinks_ref, q_ref, k0_ref, k1_ref, k2_ref, v0_ref, v1_ref, v2_ref, o_ref, *,
              first_valid, n_heads, n_kv):
    c = pl.program_id(1)
    q = q_ref[0]
    k = jnp.concatenate([k0_ref[0], k1_ref[0], k2_ref[0]], axis=0)
    v = jnp.concatenate([v0_ref[0], v1_ref[0], v2_ref[0]], axis=0)
    n_keys = k.shape[0]
    G = n_heads // n_kv
    valid = (c * CHUNK + lax.broadcasted_iota(jnp.int32, (G * CHUNK, n_keys), 1)) >= first_valid
    pieces = []
    for g in range(n_kv):
        qs = jnp.concatenate([q[:, h * HEAD_DIM:(h + 1) * HEAD_DIM] for h in range(g * G, (g + 1) * G)], axis=0)
        kg = k[:, g * HEAD_DIM:(g + 1) * HEAD_DIM]
        vg = v[:, g * HEAD_DIM:(g + 1) * HEAD_DIM]
        sink = jnp.concatenate([jnp.full((CHUNK, 1), sinks_ref[h], F32) for h in range(g * G, (g + 1) * G)], axis=0)
        s = lax.dot_general(qs, kg, _NT, preferred_element_type=F32)
        s = jnp.where(valid, s, NEG)
        m = jnp.maximum(jnp.max(s, axis=1, keepdims=True), sink)
        e = jnp.exp(s - m)
        p = e / (jnp.sum(e, axis=1, keepdims=True) + jnp.exp(sink - m))
        o = jnp.dot(p.astype(MXU_DTYPE), vg, preferred_element_type=F32)
        pieces += [o[jh * CHUNK:(jh + 1) * CHUNK] for jh in range(G)]
    o_ref[0] = jnp.concatenate(pieces, axis=1).astype(o_ref.dtype)


def _swa_attention(q, kpad, vpad, sinks, *, first_valid):
    B, T, HD = q.shape
    KD = kpad.shape[2]
    n_heads, n_kv = HD // HEAD_DIM, KD // HEAD_DIM
    kv_specs = [pl.BlockSpec((1, CHUNK, KD), functools.partial(lambda b, c, s, o: (b, c + o, 0), o=o))
                for o in range(3)]
    return pl.pallas_call(
        functools.partial(_swa_body, first_valid=first_valid, n_heads=n_heads, n_kv=n_kv),
        grid_spec=pltpu.PrefetchScalarGridSpec(
            num_scalar_prefetch=1,
            grid=(B, T // CHUNK),
            in_specs=[pl.BlockSpec((1, CHUNK, HD), lambda b, c, s: (b, c, 0))] + kv_specs + kv_specs,
            out_specs=pl.BlockSpec((1, CHUNK, HD), lambda b, c, s: (b, c, 0))),
        out_shape=jax.ShapeDtypeStruct((B, T, HD), MXU_DTYPE),
        compiler_params=_params("arbitrary", "arbitrary"),
        name="swa_attention",
    )(sinks.astype(F32), q, kpad, kpad, kpad, vpad, vpad, vpad)


def _gmm1_body(be_ref, nu_ref, x_ref, wg_ref, wu_ref, bg_ref, bu_ref, o_ref, wg_s, wu_s):
    m = pl.program_id(1)
    changed = (m == 0) | (be_ref[m] != be_ref[jnp.maximum(m - 1, 0)])

    @pl.when(changed)
    def _():
        wg_s[...] = wg_ref[0, 0].astype(MXU_DTYPE)
        wu_s[...] = wu_ref[0, 0].astype(MXU_DTYPE)

    @pl.when(m < nu_ref[0])
    def _():
        x = x_ref[...]
        x = _unpack_pairs(x) if x.dtype == jnp.int32 else x.astype(MXU_DTYPE)
        tf = wg_s.shape[1]
        sb = min(tf, 2 * LANES)
        bg, bu = bg_ref[0, 0], bu_ref[0, 0]
        for c in range(tf // sb):
            sl = slice(c * sb, (c + 1) * sb)
            g = jnp.dot(x, wg_s[:, sl], preferred_element_type=F32) + bg[:, sl]
            u = jnp.dot(x, wu_s[:, sl], preferred_element_type=F32) + bu[:, sl]
            g = jnp.minimum(g, SWIGLU_LIMIT)
            u = jnp.clip(u, -SWIGLU_LIMIT, SWIGLU_LIMIT)
            o_ref[:, sl] = (g * _sigmoid(SWIGLU_ALPHA * g) * (u + 1.0)).astype(o_ref.dtype)

    @pl.when(m >= nu_ref[0])
    def _():
        o_ref[...] = jnp.zeros(o_ref.shape, o_ref.dtype)


def _gmm2_body(be_ref, nu_ref, a_ref, w_ref, b_ref, o_ref, w_s):
    m = pl.program_id(1)
    changed = (m == 0) | (be_ref[m] != be_ref[jnp.maximum(m - 1, 0)])

    @pl.when(changed)
    def _():
        w_s[...] = w_ref[0, 0].astype(MXU_DTYPE)

    @pl.when(m < nu_ref[0])
    def _():
        o_ref[...] = jnp.dot(a_ref[...], w_s[...], preferred_element_type=F32) + b_ref[0, 0]

    @pl.when(m >= nu_ref[0])
    def _():
        o_ref[...] = jnp.zeros(o_ref.shape, o_ref.dtype)


def _moe_experts(xs, blk_e, n_used, layer, w_gu, b_gu, w_dn, b_dn, *, tm, tf=512, tn=1024):
    R, xw = xs.shape
    L, E, D, F2 = w_gu.shape
    Fh = F2 // 2
    tf = _tile(Fh, tf)
    tn = _tile(D, tn)
    nb = R // tm
    nf = Fh // tf

    def xmap(j, m, be, nu):
        return (jnp.minimum(m, nu[0] - 1), 0)

    act = pl.pallas_call(
        _gmm1_body,
        grid_spec=pltpu.PrefetchScalarGridSpec(
            num_scalar_prefetch=2,
            grid=(nf, nb),
            in_specs=[pl.BlockSpec((tm, xw), xmap),
                      pl.BlockSpec((1, 1, D, tf), lambda j, m, be, nu: (layer, be[m], 0, j)),
                      pl.BlockSpec((1, 1, D, tf), lambda j, m, be, nu: (layer, be[m], 0, nf + j)),
                      pl.BlockSpec((1, 1, 1, tf), lambda j, m, be, nu: (layer, be[m], 0, j)),
                      pl.BlockSpec((1, 1, 1, tf), lambda j, m, be, nu: (layer, be[m], 0, nf + j))],
            out_specs=pl.BlockSpec((tm, tf), lambda j, m, be, nu: (m, j)),
            scratch_shapes=[pltpu.VMEM((D, tf), MXU_DTYPE), pltpu.VMEM((D, tf), MXU_DTYPE)]),
        out_shape=jax.ShapeDtypeStruct((R, Fh), MXU_DTYPE),
        compiler_params=_params("arbitrary", "arbitrary"),
        name="moe_gate_up",
    )(blk_e, n_used, xs, w_gu, w_gu, b_gu.reshape(L, E, 1, F2), b_gu.reshape(L, E, 1, F2))

    ys = pl.pallas_call(
        _gmm2_body,
        grid_spec=pltpu.PrefetchScalarGridSpec(
            num_scalar_prefetch=2,
            grid=(D // tn, nb),
            in_specs=[pl.BlockSpec((tm, Fh), xmap),
                      pl.BlockSpec((1, 1, Fh, tn), lambda j, m, be, nu: (layer, be[m], 0, j)),
                      pl.BlockSpec((1, 1, 1, tn), lambda j, m, be, nu: (layer, be[m], 0, j))],
            out_specs=pl.BlockSpec((tm, tn), lambda j, m, be, nu: (m, j)),
            scratch_shapes=[pltpu.VMEM((Fh, tn), MXU_DTYPE)]),
        out_shape=jax.ShapeDtypeStruct((R, D), F32),
        compiler_params=_params("arbitrary", "arbitrary"),
        name="moe_down",
    )(blk_e, n_used, act, w_dn, b_dn.reshape(L, E, 1, D))
    return ys


def _combine_body(y_ref, gate_ref, res_ref, mg_ref, o_ref):
    gate = gate_ref[...]
    y = None
    for k in range(y_ref.shape[0]):
        t = gate[:, k:k + 1] * y_ref[k]
        y = t if y is None else y + t
    tm, d = y.shape
    o_ref[...] = res_ref[...] + mg_ref[...] * y.reshape(tm // CHUNK, CHUNK, d)


def _moe_combine(yk, gate, res3, mod_exp, k_gate, *, tm=256):
    K, N, D = yk.shape
    NB, C, _ = res3.shape
    tm = _tile(N, tm, C)
    return pl.pallas_call(
        _combine_body,
        grid=(N // tm,),
        in_specs=[pl.BlockSpec((K, tm, D), lambda i: (0, i, 0)),
                  pl.BlockSpec((tm, K), lambda i: (i, 0)),
                  pl.BlockSpec((tm // C, C, D), lambda i: (i, 0, 0)),
                  pl.BlockSpec((tm // C, 1, D), lambda i: (i, 0, k_gate))],
        out_specs=pl.BlockSpec((tm // C, C, D), lambda i: (i, 0, 0)),
        out_shape=jax.ShapeDtypeStruct((NB, C, D), F32),
        compiler_params=_params("arbitrary"),
        name="moe_combine",
    )(yk, gate, res3, mod_exp)


def _moe(h, hp, layer, res3, mod_exp, w_router, b_router, w_gu, b_gu, w_dn, b_dn, *, tm):
    N, D = h.shape
    E = w_router.shape[-1]
    wr = jnp.pad(w_router[layer], ((0, 0), (0, LANES - E)))
    br = jnp.pad(b_router[layer], (0, LANES - E)).reshape(1, LANES)
    logits = _mm(h, wr, [F32], tn=LANES, bias=br)[0][:, :E]
    top_logit, top_e = lax.top_k(logits, TOP_K)
    gate = jax.nn.softmax(top_logit, axis=-1)
    e_flat = top_e.reshape(-1)
    onehot = (e_flat[:, None] == jnp.arange(E, dtype=jnp.int32)[None, :]).astype(jnp.int32)
    rank = jnp.take_along_axis(jnp.cumsum(onehot, axis=0) - onehot, e_flat[:, None], axis=1)[:, 0]
    counts = jnp.sum(onehot, axis=0)
    padded = (counts + tm - 1) // tm * tm
    pend = jnp.cumsum(padded)
    dest = (pend - padded)[e_flat] + rank
    n_blocks = -(-N * TOP_K // tm) + E
    src_tok = jnp.zeros((n_blocks * tm,), jnp.int32).at[dest].set(jnp.arange(N * TOP_K, dtype=jnp.int32) // TOP_K)
    blk_start = jnp.arange(n_blocks, dtype=jnp.int32) * tm
    blk_e = jnp.minimum(jnp.sum((pend[None, :] <= blk_start[:, None]).astype(jnp.int32), axis=1), E - 1)
    n_used = (pend[-1] // tm).astype(jnp.int32).reshape(1)
    xs = hp[src_tok]
    ys = _moe_experts(xs, blk_e, n_used, layer, w_gu, b_gu, w_dn, b_dn, tm=tm)
    yk = ys[dest.reshape(N, TOP_K).T.reshape(-1)].reshape(TOP_K, N, D)
    return _moe_combine(yk, gate, res3, mod_exp, 5)


def kernel(x_prompt, x_sample, c_prompt, c_sample, cache_fox_kv, cache_fox_logf, cache_dsa_kv, cache_dsa_kidx, cache_swa_kv, w_ada, b_ada, g_mix, g_ffn, w_in_ab, b_forget, w_out_ab, w_in_c, sinks_c, w_out_c, w_router, b_router, w_gu, b_gu, w_dn, b_dn, g_final):
    Bp, Tp, D = x_prompt.shape
    Bs, Ts, _ = x_sample.shape
    depth = w_ada.shape[0]
    past_len = cache_fox_kv.shape[2]
    h_fox = cache_fox_kv.shape[4]
    kv_dsa = cache_dsa_kv.shape[4]
    d_idx = cache_dsa_kidx.shape[3]
    window = cache_swa_kv.shape[2]
    kv_swa = cache_swa_kv.shape[4]
    h_swa = sinks_c.shape[1]
    h_dsa = w_out_ab.shape[1] // HEAD_DIM - h_fox
    fox_w, dq_w, dkv_w = h_fox * HEAD_DIM, h_dsa * HEAD_DIM, kv_dsa * HEAD_DIM
    h_idx = (w_in_ab.shape[2] - 3 * fox_w - h_fox - dq_w - 2 * dkv_w - d_idx) // (d_idx + 1)
    assert Bp == 1 and Ts == CHUNK and Tp % CHUNK == 0 and past_len % CHUNK == 0
    assert d_idx == HEAD_DIM and window == 2 * CHUNK and d_idx + h_fox + h_idx <= LANES
    Mp, Ms = Bp * Tp, Bs * Ts
    M = Mp + Ms
    NB = M // CHUNK
    hd_scale = HEAD_DIM ** -0.5

    pos = jnp.concatenate([jnp.arange(Tp, dtype=jnp.int32),
                           jnp.tile(past_len + jnp.arange(Ts, dtype=jnp.int32), Bs)])
    half = HEAD_DIM // 2
    inv = ROPE_THETA ** (-jnp.arange(half, dtype=F32) / half)
    ang = pos.astype(F32)[:, None] * inv
    cos, sin = jnp.cos(ang), jnp.sin(ang)
    rope = (jnp.concatenate([cos, cos, cos, cos], axis=1), jnp.concatenate([-sin, sin, -sin, sin], axis=1))

    n_c = Bp + Bs
    c_all = jnp.pad(jnp.concatenate([c_prompt, c_sample], axis=0), ((0, -n_c % 8), (0, 0)))
    mod = _ada(c_all, w_ada, b_ada)
    blk_row = jnp.concatenate([jnp.zeros((Mp // CHUNK,), jnp.int32),
                               Bp + jnp.arange(Bs, dtype=jnp.int32)])
    x3 = jnp.concatenate([x_prompt.reshape(Mp // CHUNK, CHUNK, D), x_sample.reshape(Ms // CHUNK, CHUNK, D)], axis=0)

    fox_kv_p, fox_kv_s, logf_p, logf_s, dsa_kv_p, dsa_kv_s, kidx_p, kidx_s, swa_p, swa_s = ([] for _ in range(10))
    moe_tm = 256 if M * TOP_K >= 8192 else 64

    for l in range(depth):
        mod_exp = mod[l][blk_row][:, None, :]
        h = _norm_mod(x3, g_mix[l], mod_exp, 0, 1).reshape(M, D)
        j = l // 2
        if l % 2 == 0:
            w = w_in_ab[j]
            offs = np.cumsum([0, fox_w, fox_w, fox_w, h_fox, dq_w, dkv_w, dkv_w, h_idx * d_idx, d_idx, h_idx])
            seg = lambda a, b: w[:, offs[a]:offs[b]].astype(MXU_DTYPE)
            q_fox = _mm(h, seg(0, 1), [MXU_DTYPE], scale=hd_scale)[0]
            fkv32, fkv = _mm(h, seg(1, 3), [F32, MXU_DTYPE])
            q_dsa = _mm(h, seg(4, 5), [MXU_DTYPE], scale=hd_scale, rope=rope, rope_tiles=10 ** 6)[0]
            dkv32 = _mm(h, seg(5, 7), [F32], tn=dkv_w, rope=rope, rope_tiles=1)[0]
            qi = _mm(h, seg(7, 8), [MXU_DTYPE], scale=d_idx ** -0.5, rope=rope, rope_tiles=10 ** 6)[0]
            pad = LANES - d_idx - h_fox - h_idx
            w_small = jnp.concatenate([w[:, offs[8]:offs[9]], w[:, offs[3]:offs[4]], w[:, offs[9]:offs[10]],
                                       jnp.zeros((D, pad), w.dtype)], axis=1).astype(MXU_DTYPE)
            bf_row = jnp.pad(b_forget[j], (d_idx, LANES - d_idx - h_fox)).reshape(1, LANES)
            small = _mm_small(h, w_small, rope, bf_row, d_idx=d_idx, h_fox=h_fox, wi_scale=h_idx ** -0.5)
            kidx, logf, wi = small[:, :d_idx], small[:, d_idx:d_idx + h_fox], small[:, d_idx + h_fox:d_idx + h_fox + h_idx]

            fox_kv_p.append(fkv32[:Mp].reshape(Bp, Tp, 2, h_fox, HEAD_DIM))
            fox_kv_s.append(fkv32[Mp:].reshape(Bs, Ts, 2, h_fox, HEAD_DIM))
            logf_p.append(logf[:Mp].reshape(Bp, Tp, h_fox))
            logf_s.append(logf[Mp:].reshape(Bs, Ts, h_fox))
            dsa_kv_p.append(dkv32[:Mp].reshape(Bp, Tp, 2, kv_dsa, HEAD_DIM))
            dsa_kv_s.append(dkv32[Mp:].reshape(Bs, Ts, 2, kv_dsa, HEAD_DIM))
            kidx_p.append(kidx[:Mp].reshape(Bp, Tp, d_idx))
            kidx_s.append(kidx[Mp:].reshape(Bs, Ts, d_idx))

            cum_p = jnp.cumsum(logf_p[-1], axis=1)
            oa_p = _fox_attention(q_fox[:Mp].reshape(Bp, Tp, fox_w), fkv[:Mp].reshape(Bp, Tp, 2 * fox_w),
                                  cum_p, cum_p, tq=min(512, Tp), tk=min(512, Tp))
            kv_s = jnp.concatenate([cache_fox_kv[j].reshape(Bs, past_len, 2 * fox_w).astype(MXU_DTYPE),
                                    fkv[Mp:].reshape(Bs, Ts, 2 * fox_w)], axis=1)
            cum_s = jnp.cumsum(jnp.concatenate([cache_fox_logf[j].astype(F32), logf_s[-1]], axis=1), axis=1)
            oa_s = _fox_attention(q_fox[Mp:].reshape(Bs, Ts, fox_w), kv_s, cum_s[:, past_len:], cum_s,
                                  tq=Ts, tk=_tile(past_len, 256, CHUNK))
            dkv = dkv32.astype(MXU_DTYPE)
            kid = jnp.concatenate([kidx, kidx], axis=1).astype(MXU_DTYPE)
            ob_p = _dsa_attention(q_dsa[:Mp].reshape(Bp, Tp, dq_w), dkv[:Mp, :dkv_w].reshape(Bp, Tp, dkv_w),
                                  dkv[:Mp, dkv_w:].reshape(Bp, Tp, dkv_w), qi[:Mp].reshape(Bp, Tp, -1),
                                  kid[:Mp].reshape(Bp, Tp, LANES), wi[:Mp].reshape(Bp, Tp, h_idx),
                                  s_real=Tp, topk=min(TOPK_MAX, Tp // 4), tq=min(256, Tp), tk=min(512, Tp))
            S = past_len + Ts
            s_pad = -(-S // LANES) * LANES
            padk = lambda a: jnp.pad(a, ((0, 0), (0, s_pad - S), (0, 0)))
            pkv = cache_dsa_kv[j].reshape(Bs, past_len, 2 * dkv_w).astype(MXU_DTYPE)
            k_s = padk(jnp.concatenate([pkv[:, :, :dkv_w], dkv[Mp:, :dkv_w].reshape(Bs, Ts, dkv_w)], axis=1))
            v_s = padk(jnp.concatenate([pkv[:, :, dkv_w:], dkv[Mp:, dkv_w:].reshape(Bs, Ts, dkv_w)], axis=1))
            pki = cache_dsa_kidx[j].astype(MXU_DTYPE)
            kid_s = padk(jnp.concatenate([jnp.concatenate([pki, pki], axis=2), kid[Mp:].reshape(Bs, Ts, LANES)], axis=1))
            ob_s = _dsa_attention(q_dsa[Mp:].reshape(Bs, Ts, dq_w), k_s, v_s, qi[Mp:].reshape(Bs, Ts, -1),
                                  kid_s, wi[Mp:].reshape(Bs, Ts, h_idx),
                                  s_real=S, topk=min(TOPK_MAX, S // 4), tq=Ts, tk=LANES)
            oa = jnp.concatenate([oa_p.reshape(Mp, fox_w), oa_s.reshape(Ms, fox_w)], axis=0)
            ob = jnp.concatenate([ob_p.reshape(Mp, dq_w), ob_s.reshape(Ms, dq_w)], axis=0)
            if fox_w == dq_w:
                x3 = _mm_res([oa, ob], w_out_ab, j, x3, mod_exp, 2)
            else:
                x3 = _mm_res([jnp.concatenate([oa, ob], axis=1)], w_out_ab, j, x3, mod_exp, 2)
        else:
            w = w_in_c[j]
            qw, kw = h_swa * HEAD_DIM, kv_swa * HEAD_DIM
            q = _mm(h, w[:, :qw].astype(MXU_DTYPE), [MXU_DTYPE], scale=hd_scale, rope=rope, rope_tiles=10 ** 6)[0]
            kv32 = _mm(h, w[:, qw:].astype(MXU_DTYPE), [F32], tn=kw, rope=rope, rope_tiles=1)[0]
            kvb = kv32.astype(MXU_DTYPE)
            kv_new_p = kv32[:Mp].reshape(Bp, Tp, 2, kv_swa, HEAD_DIM)
            kv_new_s = kv32[Mp:].reshape(Bs, Ts, 2, kv_swa, HEAD_DIM)
            swa_p.append(kv_new_p[:, Tp - min(window, Tp):])
            swa_s.append(jnp.concatenate([cache_swa_kv[j], kv_new_s], axis=1)[:, Ts:])
            zpad = jnp.zeros((Bp, window, kw), MXU_DTYPE)
            o_p = _swa_attention(q[:Mp].reshape(Bp, Tp, qw),
                                 jnp.concatenate([zpad, kvb[:Mp, :kw].reshape(Bp, Tp, kw)], axis=1),
                                 jnp.concatenate([zpad, kvb[:Mp, kw:].reshape(Bp, Tp, kw)], axis=1),
                                 sinks_c[j], first_valid=window)
            past = cache_swa_kv[j].reshape(Bs, window, 2 * kw).astype(MXU_DTYPE)
            o_s = _swa_attention(q[Mp:].reshape(Bs, Ts, qw),
                                 jnp.concatenate([past[:, :, :kw], kvb[Mp:, :kw].reshape(Bs, Ts, kw)], axis=1),
                                 jnp.concatenate([past[:, :, kw:], kvb[Mp:, kw:].reshape(Bs, Ts, kw)], axis=1),
                                 sinks_c[j], first_valid=0)
            o = jnp.concatenate([o_p.reshape(Mp, qw), o_s.reshape(Ms, qw)], axis=0)
            x3 = _mm_res([o], w_out_c, j, x3, mod_exp, 2)

        h2, h2p = _norm_mod(x3, g_ffn[l], mod_exp, 3, 4, packed=True)
        x3 = _moe(h2.reshape(M, D), h2p.reshape(M, -1), l, x3, mod_exp,
                  w_router, b_router, w_gu, b_gu, w_dn, b_dn, tm=moe_tm)

    y = _final_norm(x3, g_final).reshape(M, D)
    st = lambda xs: jnp.stack(xs, axis=0)
    return (y[:Mp].reshape(Bp, Tp, D), y[Mp:].reshape(Bs, Ts, D),
            st(fox_kv_p), st(fox_kv_s), st(logf_p), st(logf_s), st(dsa_kv_p), st(dsa_kv_s),
            st(kidx_p), st(kidx_s), st(swa_p), st(swa_s))
```

```python
import functools

import numpy as np
import jax
import jax.numpy as jnp
from jax import lax
from jax.experimental import pallas as pl
from jax.experimental.pallas import tpu as pltpu

CHUNK = 64
HEAD_DIM = 64
ROPE_THETA = 10000.0
EPS = 1e-5
NEG = -1e30
TOPK_MAX = 256
TOP_K = 4
SWIGLU_ALPHA = 1.702
SWIGLU_LIMIT = 7.0

LANES = 128
F32 = jnp.float32
MXU_DTYPE = jnp.bfloat16
VMEM_LIMIT = 56 * 1024 * 1024

INT_MIN = -2 ** 31
_negbits = int(np.array(NEG, np.float32).view(np.int32))
KEY_NEG = _negbits ^ 0x7FFFFFFF

_NT = (((1,), (1,)), ((), ()))


def _params(*sem):
    return pltpu.CompilerParams(dimension_semantics=sem, vmem_limit_bytes=VMEM_LIMIT)


def _tile(n, pref, mult=LANES):
    if n <= pref:
        return n
    t = pref - pref % mult
    while t >= mult:
        if n % t == 0:
            return t
        t -= mult
    return n


def _sigmoid(x):
    return 1.0 / (1.0 + jnp.exp(-x))


def _ada_body(c_ref, w_ref, b_ref, o_ref):
    c = c_ref[...]
    a = (c * _sigmoid(c)).astype(MXU_DTYPE)
    o_ref[0] = jnp.dot(a, w_ref[0].astype(MXU_DTYPE), preferred_element_type=F32) + b_ref[0]


def _ada(c_all, w_ada, b_ada):
    L, D, N = w_ada.shape
    R = c_all.shape[0]
    tn = _tile(N, 1024)
    return pl.pallas_call(
        _ada_body,
        grid=(L, N // tn),
        in_specs=[pl.BlockSpec((R, D), lambda l, j: (0, 0)),
                  pl.BlockSpec((1, D, tn), lambda l, j: (l, 0, j)),
                  pl.BlockSpec((1, 1, tn), lambda l, j: (l, 0, j))],
        out_specs=pl.BlockSpec((1, R, tn), lambda l, j: (l, 0, j)),
        out_shape=jax.ShapeDtypeStruct((L, R, N), F32),
        compiler_params=_params("arbitrary", "arbitrary"),
        name="ada_mod",
    )(c_all, w_ada, b_ada.reshape(L, 1, N))


def _norm_mod_body(x_ref, g_ref, sh_ref, sc_ref, o_ref):
    x = x_ref[...]
    ms = jnp.mean(x * x, axis=-1, keepdims=True)
    xn = x * lax.rsqrt(ms + EPS) * g_ref[...]
    o_ref[...] = (xn * (1.0 + sc_ref[...]) + sh_ref[...]).astype(o_ref.dtype)


def _norm_body(x_ref, g_ref, o_ref):
    x = x_ref[...]
    ms = jnp.mean(x * x, axis=-1, keepdims=True)
    o_ref[...] = (x * lax.rsqrt(ms + EPS) * g_ref[...]).astype(o_ref.dtype)


def _pack_pairs(h):
    half = h.shape[-1] // 2
    bits = pltpu.bitcast(h.astype(jnp.bfloat16).astype(F32), jnp.int32)
    return bits[..., half:] | lax.shift_right_logical(bits[..., :half], 16)


def _unpack_pairs(w):
    lo = pltpu.bitcast(lax.shift_left(w, 16), F32).astype(jnp.bfloat16)
    hi = pltpu.bitcast(w & jnp.int32(-65536), F32).astype(jnp.bfloat16)
    return jnp.concatenate([lo, hi], axis=1)


def _norm_mod_pack_body(x_ref, g_ref, sh_ref, sc_ref, o_ref, p_ref):
    x = x_ref[...]
    ms = jnp.mean(x * x, axis=-1, keepdims=True)
    xn = x * lax.rsqrt(ms + EPS) * g_ref[...]
    h = xn * (1.0 + sc_ref[...]) + sh_ref[...]
    o_ref[...] = h.astype(o_ref.dtype)
    p_ref[...] = _pack_pairs(h) if p_ref.dtype == jnp.int32 else h


def _norm_mod(x3, g, mod_exp, k_shift, k_scale, packed=False):
    NB, C, D = x3.shape
    bb = _tile(NB, 8, 1)
    blk = pl.BlockSpec((bb, C, D), lambda i: (i, 0, 0))
    out_specs, out_shape, body = blk, jax.ShapeDtypeStruct((NB, C, D), MXU_DTYPE), _norm_mod_body
    if packed:
        body = _norm_mod_pack_body
        if MXU_DTYPE == jnp.bfloat16:
            out_specs = [blk, pl.BlockSpec((bb, C, D // 2), lambda i: (i, 0, 0))]
            out_shape = [out_shape, jax.ShapeDtypeStruct((NB, C, D // 2), jnp.int32)]
        else:
            out_specs = [blk, blk]
            out_shape = [out_shape, jax.ShapeDtypeStruct((NB, C, D), F32)]
    return pl.pallas_call(
        body,
        grid=(NB // bb,),
        in_specs=[blk,
                  pl.BlockSpec((1, 1, D), lambda i: (0, 0, 0)),
                  pl.BlockSpec((bb, 1, D), lambda i: (i, 0, k_shift)),
                  pl.BlockSpec((bb, 1, D), lambda i: (i, 0, k_scale))],
        out_specs=out_specs,
        out_shape=out_shape,
        compiler_params=_params("arbitrary"),
        name="norm_mod",
    )(x3, g.reshape(1, 1, D), mod_exp, mod_exp)


def _final_norm(x3, g):
    NB, C, D = x3.shape
    bb = _tile(NB, 8, 1)
    return pl.pallas_call(
        _norm_body,
        grid=(NB // bb,),
        in_specs=[pl.BlockSpec((bb, C, D), lambda i: (i, 0, 0)),
                  pl.BlockSpec((1, 1, D), lambda i: (0, 0, 0))],
        out_specs=pl.BlockSpec((bb, C, D), lambda i: (i, 0, 0)),
        out_shape=jax.ShapeDtypeStruct((NB, C, D), F32),
        compiler_params=_params("arbitrary"),
        name="final_norm",
    )(x3, g.reshape(1, 1, D))


def _rope_tile(acc, cos, sin):
    tn = acc.shape[1]
    reps = tn // LANES
    cosf = jnp.concatenate([cos] * reps, axis=1) if reps > 1 else cos
    sinf = jnp.concatenate([sin] * reps, axis=1) if reps > 1 else sin
    lane = lax.broadcasted_iota(jnp.int32, acc.shape, 1)
    first_half = (lane & (HEAD_DIM - 1)) < HEAD_DIM // 2
    rot = jnp.where(first_half, pltpu.roll(acc, tn - HEAD_DIM // 2, 1), pltpu.roll(acc, HEAD_DIM // 2, 1))
    return acc * cosf + rot * sinf


def _mm_body(*refs, scale, rope_tiles, n_col_tiles, has_rope, has_bias):
    x_ref, w_ref = refs[:2]
    idx = 2
    if has_rope:
        cos_ref, sin_ref = refs[idx:idx + 2]
        idx += 2
    if has_bias:
        b_ref = refs[idx]
        idx += 1
    out_refs = refs[idx:]
    acc = jnp.dot(x_ref[...], w_ref[...].astype(MXU_DTYPE), preferred_element_type=F32)
    if has_bias:
        acc = acc + b_ref[...]

    def emit(val):
        if scale != 1.0:
            val = val * scale
        for o in out_refs:
            o[...] = val.astype(o.dtype)

    if has_rope and rope_tiles > 0:
        if rope_tiles >= n_col_tiles:
            emit(_rope_tile(acc, cos_ref[...], sin_ref[...]))
        else:
            j = pl.program_id(1)

            @pl.when(j < rope_tiles)
            def _():
                emit(_rope_tile(acc, cos_ref[...], sin_ref[...]))

            @pl.when(j >= rope_tiles)
            def _():
                emit(acc)
    else:
        emit(acc)


def _mm(x, w, out_dtypes, *, tm=1024, tn=512, scale=1.0, rope=None, rope_tiles=0, bias=None):
    M, K = x.shape
    N = w.shape[1]
    tm = _tile(M, tm, 64)
    tn = _tile(N, tn)
    n_col = N // tn
    in_specs = [pl.BlockSpec((tm, K), lambda i, j: (i, 0)),
                pl.BlockSpec((K, tn), lambda i, j: (0, j))]
    args = [x, w]
    if rope is not None:
        in_specs += [pl.BlockSpec((tm, LANES), lambda i, j: (i, 0))] * 2
        args += list(rope)
    if bias is not None:
        in_specs.append(pl.BlockSpec((1, tn), lambda i, j: (0, j)))
        args.append(bias)
    outs = pl.pallas_call(
        functools.partial(_mm_body, scale=scale, rope_tiles=rope_tiles if rope is not None else 0,
                          n_col_tiles=n_col, has_rope=rope is not None, has_bias=bias is not None),
        grid=(M // tm, n_col),
        in_specs=in_specs,
        out_specs=[pl.BlockSpec((tm, tn), lambda i, j: (i, j)) for _ in out_dtypes],
        out_shape=[jax.ShapeDtypeStruct((M, N), dt) for dt in out_dtypes],
        compiler_params=_params("arbitrary", "arbitrary"),
        name="proj",
    )(*args)
    return outs


def _small_body(x_ref, w_ref, cos_ref, sin_ref, bf_ref, o_ref, *, d_idx, h_fox, wi_scale):
    acc = jnp.dot(x_ref[...], w_ref[...].astype(MXU_DTYPE), preferred_element_type=F32)
    roped = _rope_tile(acc, cos_ref[...], sin_ref[...])
    z = acc + bf_ref[...]
    logf = jnp.minimum(z, 0.0) - jnp.log1p(jnp.exp(-jnp.abs(z)))
    lane = lax.broadcasted_iota(jnp.int32, acc.shape, 1)
    o_ref[...] = jnp.where(lane < d_idx, roped, jnp.where(lane < d_idx + h_fox, logf, acc * wi_scale))


def _mm_small(x, w, rope, bf_row, *, d_idx, h_fox, wi_scale, tm=1024):
    M, K = x.shape
    tm = _tile(M, tm, 64)
    return pl.pallas_call(
        functools.partial(_small_body, d_idx=d_idx, h_fox=h_fox, wi_scale=wi_scale),
        grid=(M // tm,),
        in_specs=[pl.BlockSpec((tm, K), lambda i: (i, 0)),
                  pl.BlockSpec((K, LANES), lambda i: (0, 0)),
                  pl.BlockSpec((tm, LANES), lambda i: (i, 0)),
                  pl.BlockSpec((tm, LANES), lambda i: (i, 0)),
                  pl.BlockSpec((1, LANES), lambda i: (0, 0))],
        out_specs=pl.BlockSpec((tm, LANES), lambda i: (i, 0)),
        out_shape=jax.ShapeDtypeStruct((M, LANES), F32),
        compiler_params=_params("arbitrary"),
        name="proj_small",
    )(x, w, rope[0], rope[1], bf_row)


def _mm_res_body(*refs, n_pairs):
    xs = refs[:n_pairs]
    ws = refs[n_pairs:2 * n_pairs]
    res_ref, gate_ref, o_ref = refs[2 * n_pairs:]
    acc = None
    for x_ref, w_ref in zip(xs, ws):
        part = jnp.dot(x_ref[...], w_ref[0].astype(MXU_DTYPE), preferred_element_type=F32)
        acc = part if acc is None else acc + part
    tm, tn = acc.shape
    o_ref[...] = res_ref[...] + gate_ref[...] * acc.reshape(tm // CHUNK, CHUNK, tn)


def _mm_res(xs, w_all, layer, res3, mod_exp, k_gate, *, tm=1024, tn=512):
    NB, C, N = res3.shape
    M = NB * C
    tm = _tile(M, tm, C)
    tn = _tile(N, tn)
    n_col = N // tn
    kp = xs[0].shape[1]
    assert all(x.shape == (M, kp) for x in xs) and w_all.shape[1] == kp * len(xs)
    in_specs = [pl.BlockSpec((tm, kp), lambda i, j: (i, 0)) for _ in xs]
    in_specs += [pl.BlockSpec((1, kp, tn), functools.partial(lambda i, j, p: (layer, p, j), p=p))
                 for p in range(len(xs))]
    in_specs += [pl.BlockSpec((tm // C, C, tn), lambda i, j: (i, 0, j)),
                 pl.BlockSpec((tm // C, 1, tn), lambda i, j: (i, 0, k_gate * n_col + j))]
    return pl.pallas_call(
        functools.partial(_mm_res_body, n_pairs=len(xs)),
        grid=(M // tm, n_col),
        in_specs=in_specs,
        out_specs=pl.BlockSpec((tm // C, C, tn), lambda i, j: (i, 0, j)),
        out_shape=jax.ShapeDtypeStruct((NB, C, N), F32),
        compiler_params=_params("arbitrary", "arbitrary"),
        name="out_proj_res",
    )(*xs, *([w_all] * len(xs)), res3, mod_exp)


def _softmax_steps(sts, vt, carry):
    stats, ps = [], []
    for h, st in enumerate(sts):
        m, l = carry[3 * h], carry[3 * h + 1]
        m_new = jnp.maximum(m, jnp.max(st, axis=0, keepdims=True))
        alpha = jnp.exp(m - m_new)
        p = jnp.exp(st - m_new)
        stats.append((m_new, alpha * l + jnp.sum(p, axis=0, keepdims=True), alpha))
        ps.append(p.astype(MXU_DTYPE))
    new = []
    for h, p in enumerate(ps):
        m_new, l, alpha = stats[h]
        new += [m_new, l, alpha * carry[3 * h + 2] + jnp.dot(vt, p, preferred_element_type=F32)]
    return tuple(new)


def _fox_body(q_ref, k_ref, v_ref, o_ref, *, tq, tk, q_off, nq):
    q0 = q_off if nq == 1 else q_off + pl.program_id(2) * tq
    n_full = q0 // tk

    def step(kc, vt, carry, causal):
        sts = [jnp.dot(kc, q_ref[0, h], preferred_element_type=F32) for h in range(2)]
        if causal:
            row = lax.broadcasted_iota(jnp.int32, sts[0].shape, 0)
            col = lax.broadcasted_iota(jnp.int32, sts[0].shape, 1)
            sts = [jnp.where(row <= col, st, NEG) for st in sts]
        return _softmax_steps(sts, vt, carry)

    def full_chunk(c, carry):
        kc = k_ref[0, 0, pl.ds(pl.multiple_of(c * tk, tk), tk), :]
        return step(kc, v_ref[0, 0, c], carry, False)

    init = (jnp.full((1, tq), NEG, F32), jnp.zeros((1, tq), F32), jnp.zeros((LANES, tq), F32)) * 2
    carry = lax.fori_loop(0, n_full, full_chunk, init)
    k0 = n_full * tk
    kc = k_ref[0, 0, pl.ds(k0 if isinstance(k0, int) else pl.multiple_of(k0, tk), tq), :]
    carry = step(kc, v_ref[0, 0, n_full][:, :tq], carry, True)
    o0, o1 = carry[2] / carry[1], carry[5] / carry[4]
    o_ref[0, 0] = jnp.concatenate([o0[:HEAD_DIM], o1[HEAD_DIM:]], axis=0).astype(o_ref.dtype)


def _split3(x):
    def trunc(v):
        return lax.bitcast_convert_type(lax.bitcast_convert_type(v, jnp.int32) & jnp.int32(-65536), F32)
    hi = trunc(x)
    r1 = x - hi
    mid = trunc(r1)
    return [hi.astype(MXU_DTYPE), mid.astype(MXU_DTYPE), (r1 - mid).astype(MXU_DTYPE)]


def _fox_attention(q, kv, cum_q, cum_k, *, tq, tk):
    B, T, HD = q.shape
    S = kv.shape[1]
    P = HD // LANES
    q_off = S - T
    nq = T // tq
    assert T % tq == 0 and q_off % tk == 0 and (tq == tk or (nq == 1 and tq <= tk))
    nc = -(-S // tk)
    H = 2 * P
    place = np.zeros((3 * H, P * LANES), np.float32)
    ones_row = np.zeros((P * LANES,), np.float32)
    for h in range(H):
        for j in range(3):
            place[j * H + h, (h // 2) * LANES + (h % 2) * 6 + j] = 1.0
            ones_row[(h // 2) * LANES + (h % 2) * 6 + 3 + j] = 1.0
    ck3 = jnp.concatenate(_split3(-cum_k), axis=1)
    k_aug = jnp.einsum('bjs,jc->bsc', ck3, jnp.asarray(place, MXU_DTYPE), preferred_element_type=F32) + ones_row
    k_aug = k_aug.astype(MXU_DTYPE).reshape(B, S, P, LANES)
    kk = jnp.concatenate([kv[..., :HD].reshape(B, S, P, LANES), k_aug], axis=-1).transpose(0, 2, 1, 3)
    qT = q.transpose(0, 2, 1).reshape(B, P, 2, HEAD_DIM, T)
    cq3 = [c.reshape(B, P, 2, 1, T) for c in _split3(cum_q)]
    one = jnp.ones((B, P, 3, T), MXU_DTYPE)
    zrows = lambda n: jnp.zeros((B, P, n, T), MXU_DTYPE)
    heads = []
    for e in range(2):
        rows = [qT[:, :, 0], zrows(HEAD_DIM)] if e == 0 else [zrows(HEAD_DIM), qT[:, :, 1]]
        rows += ([zrows(6 * e)] if e else []) + [one] + [c[:, :, e] for c in cq3] + [zrows(LANES - 6 * e - 6)]
        heads.append(jnp.concatenate(rows, axis=2))
    qt = jnp.stack(heads, axis=2).reshape(B, H, 2 * LANES, T)
    v = jnp.pad(kv[..., HD:], ((0, 0), (0, nc * tk - S), (0, 0)))
    vt = v.reshape(B, nc, tk, P, LANES).transpose(0, 3, 1, 4, 2)
    ot = pl.pallas_call(
        functools.partial(_fox_body, tq=tq, tk=tk, q_off=q_off, nq=nq),
        grid=(B, P, nq),
        in_specs=[pl.BlockSpec((1, 2, 2 * LANES, tq), lambda b, p, i: (b, p, 0, i)),
                  pl.BlockSpec((1, 1, S, 2 * LANES), lambda b, p, i: (b, p, 0, 0)),
                  pl.BlockSpec((1, 1, nc, LANES, tk), lambda b, p, i: (b, p, 0, 0, 0))],
        out_specs=pl.BlockSpec((1, 1, LANES, tq), lambda b, p, i: (b, p, 0, i)),
        out_shape=jax.ShapeDtypeStruct((B, P, LANES, T), MXU_DTYPE),
        compiler_params=_params("arbitrary", "arbitrary", "arbitrary"),
        name="fox_attention",
    )(qt, kk, vt)
    return ot.transpose(0, 3, 1, 2).reshape(B, T, HD)


def _dsa_body(qt_ref, k_ref, vt_ref, qit_ref, kid_ref, wit_ref, o_ref, keys_scr, bias_scr, j_scr, *,
              tq, tk, q_off, s_real, topk, n_heads, n_kv, n_idx, nq, n_chunks):
    if nq == 1:
        q0 = q_off
        n_ch = min((q0 + tq + tk - 1) // tk, n_chunks)
        unscanned = float(s_real - min(n_ch * tk, s_real))
    else:
        q0 = q_off + pl.program_id(1) * tq
        n_ch = jnp.minimum((q0 + tq + tk - 1) // tk, n_chunks)
        unscanned = (s_real - jnp.minimum(n_ch * tk, s_real)).astype(F32)
    krow = lax.broadcasted_iota(jnp.int32, (tk, tq), 0)
    qchunk = (q0 + lax.broadcasted_iota(jnp.int32, (tk, tq), 1)) // CHUNK

    def admissible(kpos):
        return ((kpos // CHUNK) <= qchunk) & (kpos < s_real)

    wit = wit_ref[0]

    def score_chunk(c, _):
        kc = kid_ref[0, pl.ds(pl.multiple_of(c * tk, tk), tk), :]
        score = jnp.zeros((tk, tq), F32)
        for h in range(n_idx):
            s = jnp.dot(kc, qit_ref[0, h], preferred_element_type=F32)
            score = score + wit[h:h + 1, :] * jnp.maximum(s, 0.0)
        kpos = c * tk + krow
        sm = jnp.where(admissible(kpos), score, NEG)
        bits = pltpu.bitcast(sm, jnp.int32)
        key = jnp.where(bits < 0, bits ^ 0x7FFFFFFF, bits)
        key = jnp.where(bits == INT_MIN, 0, key)
        keys_scr[c] = jnp.where(kpos < s_real, key, INT_MIN)
        return 0

    lax.fori_loop(0, n_ch, score_chunk, 0)

    if nq == 1:
        if n_ch % 2:
            keys_scr[n_ch] = jnp.full((tk, tq), INT_MIN, jnp.int32)
    else:
        @pl.when(n_ch % 2 == 1)
        def _():
            keys_scr[n_ch] = jnp.full((tk, tq), INT_MIN, jnp.int32)

    def count(pred):
        def body(i, acc):
            for c in (2 * i, 2 * i + 1):
                x = jnp.where(pred(keys_scr[c], c), 1.0, 0.0)
                acc = acc + jnp.sum(x.reshape(tk // 8, 8, tq), axis=0)
            return acc
        acc = lax.fori_loop(0, (n_ch + 1) // 2, body, jnp.zeros((8, tq), F32))
        return jnp.sum(acc, axis=0, keepdims=True)

    kf = float(topk)

    def bit_step(it, carry):
        t, cge = carry
        cand = t + jnp.left_shift(jnp.int32(1), 31 - it)
        cnt = count(lambda kc, c: kc >= cand) + jnp.where(cand <= KEY_NEG, unscanned, 0.0)
        ok = cnt >= kf
        return jnp.where(ok, cand, t), jnp.where(ok, cnt, cge)

    thr, cge = lax.fori_loop(0, 32, bit_step,
                             (jnp.full((1, tq), INT_MIN, jnp.int32), jnp.full((1, tq), 3e38, F32)))
    cgt = count(lambda kc, c: kc > thr) + jnp.where(thr < KEY_NEG, unscanned, 0.0)
    need = kf - cgt

    n_bits = int(n_chunks * tk).bit_length()
    j_scr[...] = jnp.full((1, tq), 2 ** n_bits, jnp.int32)

    @pl.when(jnp.max(cge) > kf)
    def _():
        def j_step(it, jcur):
            cand = jcur + jnp.left_shift(jnp.int32(1), n_bits - 1 - it)
            f = count(lambda kc, c: (kc == thr) & ((c * tk + krow) < cand))
            return jnp.where(f < need, cand, jcur)
        j_scr[...] = lax.fori_loop(0, n_bits, j_step, jnp.zeros((1, tq), jnp.int32))

    jmax = j_scr[...]

    def bias_chunk(c, _):
        kc = keys_scr[c]
        kpos = c * tk + krow
        sel = (kc > thr) | ((kc == thr) & (kpos <= jmax))
        bias_scr[c] = jnp.where(sel & admissible(kpos), 0.0, NEG)
        return 0

    lax.fori_loop(0, n_ch, bias_chunk, 0)

    G2 = 2 * n_heads // n_kv
    outs = []
    for r in range(n_kv // 2):

        def attend(c, carry, r=r):
            kc = k_ref[0, pl.ds(pl.multiple_of(c * tk, tk), tk), r * LANES:(r + 1) * LANES]
            vt = vt_ref[0, r, c]
            b = bias_scr[c]
            sts = [jnp.dot(kc, qt_ref[0, r * G2 + jh], preferred_element_type=F32) + b for jh in range(G2)]
            return _softmax_steps(sts, vt, carry)

        init = (jnp.full((1, tq), NEG, F32), jnp.zeros((1, tq), F32), jnp.zeros((LANES, tq), F32)) * G2
        carry = lax.fori_loop(0, n_ch, attend, init)
        for jh in range(G2):
            kh = (2 * jh) // G2
            outs.append((carry[3 * jh + 2] / carry[3 * jh + 1])[kh * HEAD_DIM:(kh + 1) * HEAD_DIM])
    o_ref[0] = jnp.concatenate(outs, axis=0).astype(o_ref.dtype)


def _dsa_attention(q, k, v, qi, kid, wi, *, s_real, topk, tq, tk):
    B, T, HD = q.shape
    s_pad = k.shape[1]
    n_heads, n_kv, n_idx = HD // HEAD_DIM, k.shape[2] // HEAD_DIM, qi.shape[2] // HEAD_DIM
    assert n_kv % 2 == 0 and n_heads % n_kv == 0 and s_pad % tk == 0 and T % tq == 0
    G = n_heads // n_kv
    nq = T // tq
    nc = s_pad // tk
    qh = q.transpose(0, 2, 1).reshape(B, n_heads, HEAD_DIM, T)
    upper = ((np.arange(n_heads) // G) % 2 == 1)[None, :, None, None]
    zq = jnp.zeros_like(qh)
    qt = jnp.concatenate([jnp.where(upper, zq, qh), jnp.where(upper, qh, zq)], axis=2)
    qih = qi.transpose(0, 2, 1).reshape(B, n_idx, HEAD_DIM, T)
    qit = jnp.concatenate([qih, jnp.zeros_like(qih)], axis=2)
    wit = wi.transpose(0, 2, 1)
    vt = v.reshape(B, nc, tk, n_kv // 2, LANES).transpose(0, 3, 1, 4, 2)
    body = functools.partial(_dsa_body, tq=tq, tk=tk, q_off=s_real - T, s_real=s_real, topk=topk,
                             n_heads=n_heads, n_kv=n_kv, n_idx=n_idx, nq=nq, n_chunks=nc)
    ot = pl.pallas_call(
        body,
        grid=(B, nq),
        in_specs=[pl.BlockSpec((1, n_heads, LANES, tq), lambda b, i: (b, 0, 0, i)),
                  pl.BlockSpec((1, s_pad, k.shape[2]), lambda b, i: (b, 0, 0)),
                  pl.BlockSpec((1, n_kv // 2, nc, LANES, tk), lambda b, i: (b, 0, 0, 0, 0)),
                  pl.BlockSpec((1, n_idx, LANES, tq), lambda b, i: (b, 0, 0, i)),
                  pl.BlockSpec((1, s_pad, LANES), lambda b, i: (b, 0, 0)),
                  pl.BlockSpec((1, n_idx, tq), lambda b, i: (b, 0, i))],
        out_specs=pl.BlockSpec((1, HD, tq), lambda b, i: (b, 0, i)),
        out_shape=jax.ShapeDtypeStruct((B, HD, T), MXU_DTYPE),
        scratch_shapes=[pltpu.VMEM((nc + 1, tk, tq), jnp.int32),
                        pltpu.VMEM((nc, tk, tq), F32),
                        pltpu.VMEM((1, tq), jnp.int32)],
        compiler_params=_params("arbitrary", "arbitrary"),
        name="dsa_attention",
    )(qt, k, vt, qit, kid, wit)
    return ot.transpose(0, 2, 1)


def _swa_body(sinks_ref, q_ref, k0_ref, k1_ref, k2_ref, v0_ref, v1_ref, v2_ref, o_ref, *,
              first_valid, n_heads, n_kv):
    c = pl.program_id(1)
    q = q_ref[0]
    k = jnp.concatenate([k0_ref[0], k1_ref[0], k2_ref[0]], axis=0)
    v = jnp.concatenate([v0_ref[0], v1_ref[0], v2_ref[0]], axis=0)
    n_keys = k.shape[0]
    G = n_heads // n_kv
    valid = (c * CHUNK + lax.broadcasted_iota(jnp.int32, (G * CHUNK, n_keys), 1)) >= first_valid
    ss = []
    for g in range(n_kv):
        qs = jnp.concatenate([q[:, h * HEAD_DIM:(h + 1) * HEAD_DIM] for h in range(g * G, (g + 1) * G)], axis=0)
        ss.append(lax.dot_general(qs, k[:, g * HEAD_DIM:(g + 1) * HEAD_DIM], _NT, preferred_element_type=F32))
    ps = []
    for g in range(n_kv):
        sink = jnp.concatenate([jnp.full((CHUNK, 1), sinks_ref[h], F32) for h in range(g * G, (g + 1) * G)], axis=0)
        s = jnp.where(valid, ss[g], NEG)
        m = jnp.maximum(jnp.max(s, axis=1, keepdims=True), sink)
        e = jnp.exp(s - m)
        ps.append((e / (jnp.sum(e, axis=1, keepdims=True) + jnp.exp(sink - m))).astype(MXU_DTYPE))
    pieces = []
    for g in range(n_kv):
        o = jnp.dot(ps[g], v[:, g * HEAD_DIM:(g + 1) * HEAD_DIM], preferred_element_type=F32)
        pieces += [o[jh * CHUNK:(jh + 1) * CHUNK] for jh in range(G)]
    o_ref[0] = jnp.concatenate(pieces, axis=1).astype(o_ref.dtype)


def _swa_attention(q, kpad, vpad, sinks, *, first_valid):
    B, T, HD = q.shape
    KD = kpad.shape[2]
    n_heads, n_kv = HD // HEAD_DIM, KD // HEAD_DIM
    kv_specs = [pl.BlockSpec((1, CHUNK, KD), functools.partial(lambda b, c, s, o: (b, c + o, 0), o=o))
                for o in range(3)]
    return pl.pallas_call(
        functools.partial(_swa_body, first_valid=first_valid, n_heads=n_heads, n_kv=n_kv),
        grid_spec=pltpu.PrefetchScalarGridSpec(
            num_scalar_prefetch=1,
            grid=(B, T // CHUNK),
            in_specs=[pl.BlockSpec((1, CHUNK, HD), lambda b, c, s: (b, c, 0))] + kv_specs + kv_specs,
            out_specs=pl.BlockSpec((1, CHUNK, HD), lambda b, c, s: (b, c, 0))),
        out_shape=jax.ShapeDtypeStruct((B, T, HD), MXU_DTYPE),
        compiler_params=_params("arbitrary", "arbitrary"),
        name="swa_attention",
    )(sinks.astype(F32), q, kpad, kpad, kpad, vpad, vpad, vpad)


def _gmm1_body(be_ref, nu_ref, x_ref, wg_ref, wu_ref, bg_ref, bu_ref, o_ref, wg_s, wu_s):
    m = pl.program_id(1)
    changed = (m == 0) | (be_ref[m] != be_ref[jnp.maximum(m - 1, 0)])

    @pl.when(changed)
    def _():
        wg_s[...] = wg_ref[0, 0].astype(MXU_DTYPE)
        wu_s[...] = wu_ref[0, 0].astype(MXU_DTYPE)

    @pl.when(m < nu_ref[0])
    def _():
        x = x_ref[...]
        x = _unpack_pairs(x) if x.dtype == jnp.int32 else x.astype(MXU_DTYPE)
        tf = wg_s.shape[1]
        sb = min(tf, 2 * LANES)
        bg, bu = bg_ref[0, 0], bu_ref[0, 0]
        for c in range(tf // sb):
            sl = slice(c * sb, (c + 1) * sb)
            g = jnp.dot(x, wg_s[:, sl], preferred_element_type=F32) + bg[:, sl]
            u = jnp.dot(x, wu_s[:, sl], preferred_element_type=F32) + bu[:, sl]
            g = jnp.minimum(g, SWIGLU_LIMIT)
            u = jnp.clip(u, -SWIGLU_LIMIT, SWIGLU_LIMIT)
            o_ref[:, sl] = (g * _sigmoid(SWIGLU_ALPHA * g) * (u + 1.0)).astype(o_ref.dtype)

    @pl.when(m >= nu_ref[0])
    def _():
        o_ref[...] = jnp.zeros(o_ref.shape, o_ref.dtype)


def _gmm2_body(be_ref, nu_ref, a_ref, w_ref, b_ref, o_ref, w_s):
    m = pl.program_id(1)
    changed = (m == 0) | (be_ref[m] != be_ref[jnp.maximum(m - 1, 0)])

    @pl.when(changed)
    def _():
        w_s[...] = w_ref[0, 0].astype(MXU_DTYPE)

    @pl.when(m < nu_ref[0])
    def _():
        o_ref[...] = jnp.dot(a_ref[...], w_s[...], preferred_element_type=F32) + b_ref[0, 0]

    @pl.when(m >= nu_ref[0])
    def _():
        o_ref[...] = jnp.zeros(o_ref.shape, o_ref.dtype)


def _moe_experts(xs, blk_e, n_used, layer, w_gu, b_gu, w_dn, b_dn, *, tm, tf=512, tn=1024):
    R, xw = xs.shape
    L, E, D, F2 = w_gu.shape
    Fh = F2 // 2
    tf = _tile(Fh, tf)
    tn = _tile(D, tn)
    nb = R // tm
    nf = Fh // tf

    def xmap(j, m, be, nu):
        return (jnp.minimum(m, nu[0] - 1), 0)

    act = pl.pallas_call(
        _gmm1_body,
        grid_spec=pltpu.PrefetchScalarGridSpec(
            num_scalar_prefetch=2,
            grid=(nf, nb),
            in_specs=[pl.BlockSpec((tm, xw), xmap),
                      pl.BlockSpec((1, 1, D, tf), lambda j, m, be, nu: (layer, be[m], 0, j)),
                      pl.BlockSpec((1, 1, D, tf), lambda j, m, be, nu: (layer, be[m], 0, nf + j)),
                      pl.BlockSpec((1, 1, 1, tf), lambda j, m, be, nu: (layer, be[m], 0, j)),
                      pl.BlockSpec((1, 1, 1, tf), lambda j, m, be, nu: (layer, be[m], 0, nf + j))],
            out_specs=pl.BlockSpec((tm, tf), lambda j, m, be, nu: (m, j)),
            scratch_shapes=[pltpu.VMEM((D, tf), MXU_DTYPE), pltpu.VMEM((D, tf), MXU_DTYPE)]),
        out_shape=jax.ShapeDtypeStruct((R, Fh), MXU_DTYPE),
        compiler_params=_params("arbitrary", "arbitrary"),
        name="moe_gate_up",
    )(blk_e, n_used, xs, w_gu, w_gu, b_gu.reshape(L, E, 1, F2), b_gu.reshape(L, E, 1, F2))

    ys = pl.pallas_call(
        _gmm2_body,
        grid_spec=pltpu.PrefetchScalarGridSpec(
            num_scalar_prefetch=2,
            grid=(D // tn, nb),
            in_specs=[pl.BlockSpec((tm, Fh), xmap),
                      pl.BlockSpec((1, 1, Fh, tn), lambda j, m, be, nu: (layer, be[m], 0, j)),
                      pl.BlockSpec((1, 1, 1, tn), lambda j, m, be, nu: (layer, be[m], 0, j))],
            out_specs=pl.BlockSpec((tm, tn), lambda j, m, be, nu: (m, j)),
            scratch_shapes=[pltpu.VMEM((Fh, tn), MXU_DTYPE)]),
        out_shape=jax.ShapeDtypeStruct((R, D), F32),
        compiler_params=_params("arbitrary", "arbitrary"),
        name="moe_down",
    )(blk_e, n_used, act, w_dn, b_dn.reshape(L, E, 1, D))
    return ys


def _combine_body(y_ref, gate_ref, res_ref, mg_ref, o_ref):
    gate = gate_ref[...]
    y = None
    for k in range(y_ref.shape[0]):
        t = gate[:, k:k + 1] * y_ref[k]
        y = t if y is None else y + t
    tm, d = y.shape
    o_ref[...] = res_ref[...] + mg_ref[...] * y.reshape(tm // CHUNK, CHUNK, d)


def _moe_combine(yk, gate, res3, mod_exp, k_gate, *, tm=256):
    K, N, D = yk.shape
    NB, C, _ = res3.shape
    tm = _tile(N, tm, C)
    return pl.pallas_call(
        _combine_body,
        grid=(N // tm,),
        in_specs=[pl.BlockSpec((K, tm, D), lambda i: (0, i, 0)),
                  pl.BlockSpec((tm, K), lambda i: (i, 0)),
                  pl.BlockSpec((tm // C, C, D), lambda i: (i, 0, 0)),
                  pl.BlockSpec((tm // C, 1, D), lambda i: (i, 0, k_gate))],
        out_specs=pl.BlockSpec((tm // C, C, D), lambda i: (i, 0, 0)),
        out_shape=jax.ShapeDtypeStruct((NB, C, D), F32),
        compiler_params=_params("arbitrary"),
        name="moe_combine",
    )(yk, gate, res3, mod_exp)


def _moe(h, hp, layer, res3, mod_exp, w_router, b_router, w_gu, b_gu, w_dn, b_dn, *, tm):
    N, D = h.shape
    E = w_router.shape[-1]
    wr = jnp.pad(w_router[layer], ((0, 0), (0, LANES - E)))
    br = jnp.pad(b_router[layer], (0, LANES - E)).reshape(1, LANES)
    logits = _mm(h, wr, [F32], tn=LANES, bias=br)[0][:, :E]
    top_logit, top_e = lax.top_k(logits, TOP_K)
    gate = jax.nn.softmax(top_logit, axis=-1)
    e_flat = top_e.reshape(-1)
    onehot = (e_flat[:, None] == jnp.arange(E, dtype=jnp.int32)[None, :]).astype(jnp.int32)
    rank = jnp.take_along_axis(jnp.cumsum(onehot, axis=0) - onehot, e_flat[:, None], axis=1)[:, 0]
    counts = jnp.sum(onehot, axis=0)
    padded = (counts + tm - 1) // tm * tm
    pend = jnp.cumsum(padded)
    dest = (pend - padded)[e_flat] + rank
    n_blocks = -(-N * TOP_K // tm) + E
    src_tok = jnp.zeros((n_blocks * tm,), jnp.int32).at[dest].set(jnp.arange(N * TOP_K, dtype=jnp.int32) // TOP_K)
    blk_start = jnp.arange(n_blocks, dtype=jnp.int32) * tm
    blk_e = jnp.minimum(jnp.sum((pend[None, :] <= blk_start[:, None]).astype(jnp.int32), axis=1), E - 1)
    n_used = (pend[-1] // tm).astype(jnp.int32).reshape(1)
    xs = hp[src_tok]
    ys = _moe_experts(xs, blk_e, n_used, layer, w_gu, b_gu, w_dn, b_dn, tm=tm)
    yk = ys[dest.reshape(N, TOP_K).T.reshape(-1)].reshape(TOP_K, N, D)
    return _moe_combine(yk, gate, res3, mod_exp, 5)


def kernel(x_prompt, x_sample, c_prompt, c_sample, cache_fox_kv, cache_fox_logf, cache_dsa_kv, cache_dsa_kidx, cache_swa_kv, w_ada, b_ada, g_mix, g_ffn, w_in_ab, b_forget, w_out_ab, w_in_c, sinks_c, w_out_c, w_router, b_router, w_gu, b_gu, w_dn, b_dn, g_final):
    Bp, Tp, D = x_prompt.shape
    Bs, Ts, _ = x_sample.shape
    depth = w_ada.shape[0]
    past_len = cache_fox_kv.shape[2]
    h_fox = cache_fox_kv.shape[4]
    kv_dsa = cache_dsa_kv.shape[4]
    d_idx = cache_dsa_kidx.shape[3]
    window = cache_swa_kv.shape[2]
    kv_swa = cache_swa_kv.shape[4]
    h_swa = sinks_c.shape[1]
    h_dsa = w_out_ab.shape[1] // HEAD_DIM - h_fox
    fox_w, dq_w, dkv_w = h_fox * HEAD_DIM, h_dsa * HEAD_DIM, kv_dsa * HEAD_DIM
    h_idx = (w_in_ab.shape[2] - 3 * fox_w - h_fox - dq_w - 2 * dkv_w - d_idx) // (d_idx + 1)
    assert Bp == 1 and Ts == CHUNK and Tp % CHUNK == 0 and past_len % CHUNK == 0
    assert d_idx == HEAD_DIM and window == 2 * CHUNK and d_idx + h_fox + h_idx <= LANES
    Mp, Ms = Bp * Tp, Bs * Ts
    M = Mp + Ms
    NB = M // CHUNK
    hd_scale = HEAD_DIM ** -0.5

    pos = jnp.concatenate([jnp.arange(Tp, dtype=jnp.int32),
                           jnp.tile(past_len + jnp.arange(Ts, dtype=jnp.int32), Bs)])
    half = HEAD_DIM // 2
    inv = ROPE_THETA ** (-jnp.arange(half, dtype=F32) / half)
    ang = pos.astype(F32)[:, None] * inv
    cos, sin = jnp.cos(ang), jnp.sin(ang)
    rope = (jnp.concatenate([cos, cos, cos, cos], axis=1), jnp.concatenate([-sin, sin, -sin, sin], axis=1))

    n_c = Bp + Bs
    c_all = jnp.pad(jnp.concatenate([c_prompt, c_sample], axis=0), ((0, -n_c % 8), (0, 0)))
    mod = _ada(c_all, w_ada, b_ada)
    blk_row = jnp.concatenate([jnp.zeros((Mp // CHUNK,), jnp.int32),
                               Bp + jnp.arange(Bs, dtype=jnp.int32)])
    x3 = jnp.concatenate([x_prompt.reshape(Mp // CHUNK, CHUNK, D), x_sample.reshape(Ms // CHUNK, CHUNK, D)], axis=0)

    fox_kv_p, fox_kv_s, logf_p, logf_s, dsa_kv_p, dsa_kv_s, kidx_p, kidx_s, swa_p, swa_s = ([] for _ in range(10))
    moe_tm = 512 if M * TOP_K >= 16384 else 64

    for l in range(depth):
        mod_exp = mod[l][blk_row][:, None, :]
        h = _norm_mod(x3, g_mix[l], mod_exp, 0, 1).reshape(M, D)
        j = l // 2
        if l % 2 == 0:
            w = w_in_ab[j]
            offs = np.cumsum([0, fox_w, fox_w, fox_w, h_fox, dq_w, dkv_w, dkv_w, h_idx * d_idx, d_idx, h_idx])
            seg = lambda a, b: w[:, offs[a]:offs[b]].astype(MXU_DTYPE)
            q_fox = _mm(h, seg(0, 1), [MXU_DTYPE], scale=hd_scale)[0]
            fkv32, fkv = _mm(h, seg(1, 3), [F32, MXU_DTYPE])
            q_dsa = _mm(h, seg(4, 5), [MXU_DTYPE], scale=hd_scale, rope=rope, rope_tiles=10 ** 6)[0]
            dkv32 = _mm(h, seg(5, 7), [F32], tn=dkv_w, rope=rope, rope_tiles=1)[0]
            qi = _mm(h, seg(7, 8), [MXU_DTYPE], scale=d_idx ** -0.5, rope=rope, rope_tiles=10 ** 6)[0]
            pad = LANES - d_idx - h_fox - h_idx
            w_small = jnp.concatenate([w[:, offs[8]:offs[9]], w[:, offs[3]:offs[4]], w[:, offs[9]:offs[10]],
                                       jnp.zeros((D, pad), w.dtype)], axis=1).astype(MXU_DTYPE)
            bf_row = jnp.pad(b_forget[j], (d_idx, LANES - d_idx - h_fox)).reshape(1, LANES)
            small = _mm_small(h, w_small, rope, bf_row, d_idx=d_idx, h_fox=h_fox, wi_scale=h_idx ** -0.5)
            kidx, logf, wi = small[:, :d_idx], small[:, d_idx:d_idx + h_fox], small[:, d_idx + h_fox:d_idx + h_fox + h_idx]

            fox_kv_p.append(fkv32[:Mp].reshape(Bp, Tp, 2, h_fox, HEAD_DIM))
            fox_kv_s.append(fkv32[Mp:].reshape(Bs, Ts, 2, h_fox, HEAD_DIM))
            logf_p.append(logf[:Mp].reshape(Bp, Tp, h_fox))
            logf_s.append(logf[Mp:].reshape(Bs, Ts, h_fox))
            dsa_kv_p.append(dkv32[:Mp].reshape(Bp, Tp, 2, kv_dsa, HEAD_DIM))
            dsa_kv_s.append(dkv32[Mp:].reshape(Bs, Ts, 2, kv_dsa, HEAD_DIM))
            kidx_p.append(kidx[:Mp].reshape(Bp, Tp, d_idx))
            kidx_s.append(kidx[Mp:].reshape(Bs, Ts, d_idx))

            cum_p = jnp.cumsum(logf_p[-1].transpose(0, 2, 1), axis=2)
            oa_p = _fox_attention(q_fox[:Mp].reshape(Bp, Tp, fox_w), fkv[:Mp].reshape(Bp, Tp, 2 * fox_w),
                                  cum_p, cum_p, tq=min(512, Tp), tk=min(512, Tp))
            kv_s = jnp.concatenate([cache_fox_kv[j].reshape(Bs, past_len, 2 * fox_w).astype(MXU_DTYPE),
                                    fkv[Mp:].reshape(Bs, Ts, 2 * fox_w)], axis=1)
            cum_s = jnp.cumsum(jnp.concatenate([cache_fox_logf[j].astype(F32), logf_s[-1]], axis=1).transpose(0, 2, 1),
                               axis=2)
            oa_s = _fox_attention(q_fox[Mp:].reshape(Bs, Ts, fox_w), kv_s, cum_s[:, :, past_len:], cum_s,
                                  tq=Ts, tk=_tile(past_len, 256, CHUNK))
            dkv = dkv32.astype(MXU_DTYPE)
            kid = jnp.concatenate([kidx, kidx], axis=1).astype(MXU_DTYPE)
            ob_p = _dsa_attention(q_dsa[:Mp].reshape(Bp, Tp, dq_w), dkv[:Mp, :dkv_w].reshape(Bp, Tp, dkv_w),
                                  dkv[:Mp, dkv_w:].reshape(Bp, Tp, dkv_w), qi[:Mp].reshape(Bp, Tp, -1),
                                  kid[:Mp].reshape(Bp, Tp, LANES), wi[:Mp].reshape(Bp, Tp, h_idx),
                                  s_real=Tp, topk=min(TOPK_MAX, Tp // 4), tq=min(256, Tp), tk=min(512, Tp))
            S = past_len + Ts
            s_pad = -(-S // LANES) * LANES
            padk = lambda a: jnp.pad(a, ((0, 0), (0, s_pad - S), (0, 0)))
            pkv = cache_dsa_kv[j].reshape(Bs, past_len, 2 * dkv_w).astype(MXU_DTYPE)
            k_s = padk(jnp.concatenate([pkv[:, :, :dkv_w], dkv[Mp:, :dkv_w].reshape(Bs, Ts, dkv_w)], axis=1))
            v_s = padk(jnp.concatenate([pkv[:, :, dkv_w:], dkv[Mp:, dkv_w:].reshape(Bs, Ts, dkv_w)], axis=1))
            pki = cache_dsa_kidx[j].astype(MXU_DTYPE)
            kid_s = padk(jnp.concatenate([jnp.concatenate([pki, pki], axis=2), kid[Mp:].reshape(Bs, Ts, LANES)], axis=1))
            ob_s = _dsa_attention(q_dsa[Mp:].reshape(Bs, Ts, dq_w), k_s, v_s, qi[Mp:].reshape(Bs, Ts, -1),
                                  kid_s, wi[Mp:].reshape(Bs, Ts, h_idx),
                                  s_real=S, topk=min(TOPK_MAX, S // 4), tq=Ts, tk=LANES)
            oa = jnp.concatenate([oa_p.reshape(Mp, fox_w), oa_s.reshape(Ms, fox_w)], axis=0)
            ob = jnp.concatenate([ob_p.reshape(Mp, dq_w), ob_s.reshape(Ms, dq_w)], axis=0)
            if fox_w == dq_w:
                x3 = _mm_res([oa, ob], w_out_ab, j, x3, mod_exp, 2)
            else:
                x3 = _mm_res([jnp.concatenate([oa, ob], axis=1)], w_out_ab, j, x3, mod_exp, 2)
        else:
            w = w_in_c[j]
            qw, kw = h_swa * HEAD_DIM, kv_swa * HEAD_DIM
            q = _mm(h, w[:, :qw].astype(MXU_DTYPE), [MXU_DTYPE], scale=hd_scale, rope=rope, rope_tiles=10 ** 6)[0]
            kv32 = _mm(h, w[:, qw:].astype(MXU_DTYPE), [F32], tn=kw, rope=rope, rope_tiles=1)[0]
            kvb = kv32.astype(MXU_DTYPE)
            kv_new_p = kv32[:Mp].reshape(Bp, Tp, 2, kv_swa, HEAD_DIM)
            kv_new_s = kv32[Mp:].reshape(Bs, Ts, 2, kv_swa, HEAD_DIM)
            swa_p.append(kv_new_p[:, Tp - min(window, Tp):])
            swa_s.append(jnp.concatenate([cache_swa_kv[j], kv_new_s], axis=1)[:, Ts:])
            zpad = jnp.zeros((Bp, window, kw), MXU_DTYPE)
            o_p = _swa_attention(q[:Mp].reshape(Bp, Tp, qw),
                                 jnp.concatenate([zpad, kvb[:Mp, :kw].reshape(Bp, Tp, kw)], axis=1),
                                 jnp.concatenate([zpad, kvb[:Mp, kw:].reshape(Bp, Tp, kw)], axis=1),
                                 sinks_c[j], first_valid=window)
            past = cache_swa_kv[j].reshape(Bs, window, 2 * kw).astype(MXU_DTYPE)
            o_s = _swa_attention(q[Mp:].reshape(Bs, Ts, qw),
                                 jnp.concatenate([past[:, :, :kw], kvb[Mp:, :kw].reshape(Bs, Ts, kw)], axis=1),
                                 jnp.concatenate([past[:, :, kw:], kvb[Mp:, kw:].reshape(Bs, Ts, kw)], axis=1),
                                 sinks_c[j], first_valid=0)
            o = jnp.concatenate([o_p.reshape(Mp, qw), o_s.reshape(Ms, qw)], axis=0)
            x3 = _mm_res([o], w_out_c, j, x3, mod_exp, 2)

        h2, h2p = _norm_mod(x3, g_ffn[l], mod_exp, 3, 4, packed=True)
        x3 = _moe(h2.reshape(M, D), h2p.reshape(M, -1), l, x3, mod_exp,
                  w_router, b_router, w_gu, b_gu, w_dn, b_dn, tm=moe_tm)

    y = _final_norm(x3, g_final).reshape(M, D)
    st = lambda xs: jnp.stack(xs, axis=0)
    return (y[:Mp].reshape(Bp, Tp, D), y[Mp:].reshape(Bs, Ts, D),
            st(fox_kv_p), st(fox_kv_s), st(logf_p), st(logf_s), st(dsa_kv_p), st(dsa_kv_s),
            st(kidx_p), st(kidx_s), st(swa_p), st(swa_s))
```

```python
import functools

import numpy as np
import jax
import jax.numpy as jnp
from jax import lax
from jax.experimental import pallas as pl
from jax.experimental.pallas import tpu as pltpu

CHUNK = 64
HEAD_DIM = 64
ROPE_THETA = 10000.0
EPS = 1e-5
NEG = -1e30
TOPK_MAX = 256
TOP_K = 4
SWIGLU_ALPHA = 1.702
SWIGLU_LIMIT = 7.0

LANES = 128
F32 = jnp.float32
MXU_DTYPE = jnp.bfloat16
VMEM_LIMIT = 56 * 1024 * 1024

INT_MIN = -2 ** 31
_negbits = int(np.array(NEG, np.float32).view(np.int32))
KEY_NEG = _negbits ^ 0x7FFFFFFF

_NT = (((1,), (1,)), ((), ()))


def _params(*sem):
    return pltpu.CompilerParams(dimension_semantics=sem, vmem_limit_bytes=VMEM_LIMIT)


def _tile(n, pref, mult=LANES):
    if n <= pref:
        return n
    t = pref - pref % mult
    while t >= mult:
        if n % t == 0:
            return t
        t -= mult
    return n


def _sigmoid(x):
    return 1.0 / (1.0 + jnp.exp(-x))


def _ada_body(c_ref, w_ref, b_ref, o_ref):
    c = c_ref[...]
    a = (c * _sigmoid(c)).astype(MXU_DTYPE)
    o_ref[0] = jnp.dot(a, w_ref[0].astype(MXU_DTYPE), preferred_element_type=F32) + b_ref[0]


def _ada(c_all, w_ada, b_ada):
    L, D, N = w_ada.shape
    R = c_all.shape[0]
    tn = _tile(N, 1024)
    return pl.pallas_call(
        _ada_body,
        grid=(L, N // tn),
        in_specs=[pl.BlockSpec((R, D), lambda l, j: (0, 0)),
                  pl.BlockSpec((1, D, tn), lambda l, j: (l, 0, j)),
                  pl.BlockSpec((1, 1, tn), lambda l, j: (l, 0, j))],
        out_specs=pl.BlockSpec((1, R, tn), lambda l, j: (l, 0, j)),
        out_shape=jax.ShapeDtypeStruct((L, R, N), F32),
        compiler_params=_params("arbitrary", "arbitrary"),
        name="ada_mod",
    )(c_all, w_ada, b_ada.reshape(L, 1, N))


def _norm_mod_body(x_ref, g_ref, sh_ref, sc_ref, o_ref):
    x = x_ref[...]
    ms = jnp.mean(x * x, axis=-1, keepdims=True)
    xn = x * lax.rsqrt(ms + EPS) * g_ref[...]
    o_ref[...] = (xn * (1.0 + sc_ref[...]) + sh_ref[...]).astype(o_ref.dtype)


def _norm_body(x_ref, g_ref, o_ref):
    x = x_ref[...]
    ms = jnp.mean(x * x, axis=-1, keepdims=True)
    o_ref[...] = (x * lax.rsqrt(ms + EPS) * g_ref[...]).astype(o_ref.dtype)


def _pack_pairs(h):
    half = h.shape[-1] // 2
    bits = pltpu.bitcast(h.astype(jnp.bfloat16).astype(F32), jnp.int32)
    return bits[..., half:] | lax.shift_right_logical(bits[..., :half], 16)


def _unpack_pairs(w):
    lo = pltpu.bitcast(lax.shift_left(w, 16), F32).astype(jnp.bfloat16)
    hi = pltpu.bitcast(w & jnp.int32(-65536), F32).astype(jnp.bfloat16)
    return jnp.concatenate([lo, hi], axis=1)


def _norm_mod_pack_body(x_ref, g_ref, sh_ref, sc_ref, o_ref, p_ref):
    x = x_ref[...]
    ms = jnp.mean(x * x, axis=-1, keepdims=True)
    xn = x * lax.rsqrt(ms + EPS) * g_ref[...]
    h = xn * (1.0 + sc_ref[...]) + sh_ref[...]
    o_ref[...] = h.astype(o_ref.dtype)
    p_ref[...] = _pack_pairs(h) if p_ref.dtype == jnp.int32 else h


def _norm_mod(x3, g, mod_exp, k_shift, k_scale, packed=False):
    NB, C, D = x3.shape
    bb = _tile(NB, 8, 1)
    blk = pl.BlockSpec((bb, C, D), lambda i: (i, 0, 0))
    out_specs, out_shape, body = blk, jax.ShapeDtypeStruct((NB, C, D), MXU_DTYPE), _norm_mod_body
    if packed:
        body = _norm_mod_pack_body
        if MXU_DTYPE == jnp.bfloat16:
            out_specs = [blk, pl.BlockSpec((bb, C, D // 2), lambda i: (i, 0, 0))]
            out_shape = [out_shape, jax.ShapeDtypeStruct((NB, C, D // 2), jnp.int32)]
        else:
            out_specs = [blk, blk]
            out_shape = [out_shape, jax.ShapeDtypeStruct((NB, C, D), F32)]
    return pl.pallas_call(
        body,
        grid=(NB // bb,),
        in_specs=[blk,
                  pl.BlockSpec((1, 1, D), lambda i: (0, 0, 0)),
                  pl.BlockSpec((bb, 1, D), lambda i: (i, 0, k_shift)),
                  pl.BlockSpec((bb, 1, D), lambda i: (i, 0, k_scale))],
        out_specs=out_specs,
        out_shape=out_shape,
        compiler_params=_params("arbitrary"),
        name="norm_mod",
    )(x3, g.reshape(1, 1, D), mod_exp, mod_exp)


def _final_norm(x3, g):
    NB, C, D = x3.shape
    bb = _tile(NB, 8, 1)
    return pl.pallas_call(
        _norm_body,
        grid=(NB // bb,),
        in_specs=[pl.BlockSpec((bb, C, D), lambda i: (i, 0, 0)),
                  pl.BlockSpec((1, 1, D), lambda i: (0, 0, 0))],
        out_specs=pl.BlockSpec((bb, C, D), lambda i: (i, 0, 0)),
        out_shape=jax.ShapeDtypeStruct((NB, C, D), F32),
        compiler_params=_params("arbitrary"),
        name="final_norm",
    )(x3, g.reshape(1, 1, D))


def _rope_tile(acc, cos, sin):
    tn = acc.shape[1]
    reps = tn // LANES
    cosf = jnp.concatenate([cos] * reps, axis=1) if reps > 1 else cos
    sinf = jnp.concatenate([sin] * reps, axis=1) if reps > 1 else sin
    lane = lax.broadcasted_iota(jnp.int32, acc.shape, 1)
    first_half = (lane & (HEAD_DIM - 1)) < HEAD_DIM // 2
    rot = jnp.where(first_half, pltpu.roll(acc, tn - HEAD_DIM // 2, 1), pltpu.roll(acc, HEAD_DIM // 2, 1))
    return acc * cosf + rot * sinf


def _mm_body(*refs, scale, rope_tiles, n_col_tiles, has_rope, has_bias):
    x_ref, w_ref = refs[:2]
    idx = 2
    if has_rope:
        cos_ref, sin_ref = refs[idx:idx + 2]
        idx += 2
    if has_bias:
        b_ref = refs[idx]
        idx += 1
    out_refs = refs[idx:]
    acc = jnp.dot(x_ref[...], w_ref[...].astype(MXU_DTYPE), preferred_element_type=F32)
    if has_bias:
        acc = acc + b_ref[...]

    def emit(val):
        if scale != 1.0:
            val = val * scale
        for o in out_refs:
            o[...] = val.astype(o.dtype)

    if has_rope and rope_tiles > 0:
        if rope_tiles >= n_col_tiles:
            emit(_rope_tile(acc, cos_ref[...], sin_ref[...]))
        else:
            j = pl.program_id(1)

            @pl.when(j < rope_tiles)
            def _():
                emit(_rope_tile(acc, cos_ref[...], sin_ref[...]))

            @pl.when(j >= rope_tiles)
            def _():
                emit(acc)
    else:
        emit(acc)


def _mm(x, w, out_dtypes, *, tm=1024, tn=512, scale=1.0, rope=None, rope_tiles=0, bias=None):
    M, K = x.shape
    N = w.shape[1]
    tm = _tile(M, tm, 64)
    tn = _tile(N, tn)
    n_col = N // tn
    in_specs = [pl.BlockSpec((tm, K), lambda i, j: (i, 0)),
                pl.BlockSpec((K, tn), lambda i, j: (0, j))]
    args = [x, w]
    if rope is not None:
        in_specs += [pl.BlockSpec((tm, LANES), lambda i, j: (i, 0))] * 2
        args += list(rope)
    if bias is not None:
        in_specs.append(pl.BlockSpec((1, tn), lambda i, j: (0, j)))
        args.append(bias)
    outs = pl.pallas_call(
        functools.partial(_mm_body, scale=scale, rope_tiles=rope_tiles if rope is not None else 0,
                          n_col_tiles=n_col, has_rope=rope is not None, has_bias=bias is not None),
        grid=(M // tm, n_col),
        in_specs=in_specs,
        out_specs=[pl.BlockSpec((tm, tn), lambda i, j: (i, j)) for _ in out_dtypes],
        out_shape=[jax.ShapeDtypeStruct((M, N), dt) for dt in out_dtypes],
        compiler_params=_params("arbitrary", "arbitrary"),
        name="proj",
    )(*args)
    return outs


def _small_body(x_ref, w_ref, cos_ref, sin_ref, bf_ref, o_ref, *, d_idx, h_fox, wi_scale):
    acc = jnp.dot(x_ref[...], w_ref[...].astype(MXU_DTYPE), preferred_element_type=F32)
    roped = _rope_tile(acc, cos_ref[...], sin_ref[...])
    z = acc + bf_ref[...]
    logf = jnp.minimum(z, 0.0) - jnp.log1p(jnp.exp(-jnp.abs(z)))
    lane = lax.broadcasted_iota(jnp.int32, acc.shape, 1)
    o_ref[...] = jnp.where(lane < d_idx, roped, jnp.where(lane < d_idx + h_fox, logf, acc * wi_scale))


def _mm_small(x, w, rope, bf_row, *, d_idx, h_fox, wi_scale, tm=1024):
    M, K = x.shape
    tm = _tile(M, tm, 64)
    return pl.pallas_call(
        functools.partial(_small_body, d_idx=d_idx, h_fox=h_fox, wi_scale=wi_scale),
        grid=(M // tm,),
        in_specs=[pl.BlockSpec((tm, K), lambda i: (i, 0)),
                  pl.BlockSpec((K, LANES), lambda i: (0, 0)),
                  pl.BlockSpec((tm, LANES), lambda i: (i, 0)),
                  pl.BlockSpec((tm, LANES), lambda i: (i, 0)),
                  pl.BlockSpec((1, LANES), lambda i: (0, 0))],
        out_specs=pl.BlockSpec((tm, LANES), lambda i: (i, 0)),
        out_shape=jax.ShapeDtypeStruct((M, LANES), F32),
        compiler_params=_params("arbitrary"),
        name="proj_small",
    )(x, w, rope[0], rope[1], bf_row)


def _mm_res_body(*refs, n_pairs):
    xs = refs[:n_pairs]
    ws = refs[n_pairs:2 * n_pairs]
    res_ref, gate_ref, o_ref = refs[2 * n_pairs:]
    acc = None
    for x_ref, w_ref in zip(xs, ws):
        part = jnp.dot(x_ref[...], w_ref[0].astype(MXU_DTYPE), preferred_element_type=F32)
        acc = part if acc is None else acc + part
    tm, tn = acc.shape
    o_ref[...] = res_ref[...] + gate_ref[...] * acc.reshape(tm // CHUNK, CHUNK, tn)


def _mm_res(xs, w_all, layer, res3, mod_exp, k_gate, *, tm=1024, tn=512):
    NB, C, N = res3.shape
    M = NB * C
    tm = _tile(M, tm, C)
    tn = _tile(N, tn)
    n_col = N // tn
    kp = xs[0].shape[1]
    assert all(x.shape == (M, kp) for x in xs) and w_all.shape[1] == kp * len(xs)
    in_specs = [pl.BlockSpec((tm, kp), lambda i, j: (i, 0)) for _ in xs]
    in_specs += [pl.BlockSpec((1, kp, tn), functools.partial(lambda i, j, p: (layer, p, j), p=p))
                 for p in range(len(xs))]
    in_specs += [pl.BlockSpec((tm // C, C, tn), lambda i, j: (i, 0, j)),
                 pl.BlockSpec((tm // C, 1, tn), lambda i, j: (i, 0, k_gate * n_col + j))]
    return pl.pallas_call(
        functools.partial(_mm_res_body, n_pairs=len(xs)),
        grid=(M // tm, n_col),
        in_specs=in_specs,
        out_specs=pl.BlockSpec((tm // C, C, tn), lambda i, j: (i, 0, j)),
        out_shape=jax.ShapeDtypeStruct((NB, C, N), F32),
        compiler_params=_params("arbitrary", "arbitrary"),
        name="out_proj_res",
    )(*xs, *([w_all] * len(xs)), res3, mod_exp)


def _softmax_steps(sts, vt, carry):
    stats, ps = [], []
    for h, st in enumerate(sts):
        m, l = carry[3 * h], carry[3 * h + 1]
        m_new = jnp.maximum(m, jnp.max(st, axis=0, keepdims=True))
        alpha = jnp.exp(m - m_new)
        p = jnp.exp(st - m_new)
        stats.append((m_new, alpha * l + jnp.sum(p, axis=0, keepdims=True), alpha))
        ps.append(p.astype(MXU_DTYPE))
    new = []
    for h, p in enumerate(ps):
        m_new, l, alpha = stats[h]
        vth = vt[h] if isinstance(vt, (list, tuple)) else vt
        new += [m_new, l, alpha * carry[3 * h + 2] + jnp.dot(vth, p, preferred_element_type=F32)]
    return tuple(new)


def _fox_body(q_ref, k_ref, v_ref, o_ref, *, tq, tk, q_off, nq, npp):
    q0 = q_off if nq == 1 else q_off + pl.program_id(2) * tq
    n_full = q0 // tk
    nh = 2 * npp

    def step(kcs, vts, carry, causal):
        sts = [jnp.dot(kcs[h // 2], q_ref[0, h], preferred_element_type=F32) for h in range(nh)]
        if causal:
            row = lax.broadcasted_iota(jnp.int32, sts[0].shape, 0)
            col = lax.broadcasted_iota(jnp.int32, sts[0].shape, 1)
            sts = [jnp.where(row <= col, st, NEG) for st in sts]
        return _softmax_steps(sts, [vts[h // 2] for h in range(nh)], carry)

    def full_chunk(c, carry):
        k0 = pl.multiple_of(c * tk, tk)
        return step([k_ref[0, j, pl.ds(k0, tk), :] for j in range(npp)], [v_ref[0, j, c] for j in range(npp)],
                    carry, False)

    init = (jnp.full((1, tq), NEG, F32), jnp.zeros((1, tq), F32), jnp.zeros((LANES, tq), F32)) * nh
    carry = lax.fori_loop(0, n_full, full_chunk, init)
    k0 = n_full * tk
    k0 = k0 if isinstance(k0, int) else pl.multiple_of(k0, tk)
    carry = step([k_ref[0, j, pl.ds(k0, tq), :] for j in range(npp)],
                 [v_ref[0, j, n_full][:, :tq] for j in range(npp)], carry, True)
    for j in range(npp):
        o0 = carry[6 * j + 2] / carry[6 * j + 1]
        o1 = carry[6 * j + 5] / carry[6 * j + 4]
        o_ref[0, j] = jnp.concatenate([o0[:HEAD_DIM], o1[HEAD_DIM:]], axis=0).astype(o_ref.dtype)


def _split3(x):
    def trunc(v):
        return lax.bitcast_convert_type(lax.bitcast_convert_type(v, jnp.int32) & jnp.int32(-65536), F32)
    hi = trunc(x)
    r1 = x - hi
    mid = trunc(r1)
    return [hi.astype(MXU_DTYPE), mid.astype(MXU_DTYPE), (r1 - mid).astype(MXU_DTYPE)]


def _fox_attention(q, kv, cum_q, cum_k, *, tq, tk):
    B, T, HD = q.shape
    S = kv.shape[1]
    P = HD // LANES
    q_off = S - T
    nq = T // tq
    assert T % tq == 0 and q_off % tk == 0 and (tq == tk or (nq == 1 and tq <= tk))
    nc = -(-S // tk)
    H = 2 * P
    place = np.zeros((3 * H, P * LANES), np.float32)
    ones_row = np.zeros((P * LANES,), np.float32)
    for h in range(H):
        for j in range(3):
            place[j * H + h, (h // 2) * LANES + (h % 2) * 6 + j] = 1.0
            ones_row[(h // 2) * LANES + (h % 2) * 6 + 3 + j] = 1.0
    ck3 = jnp.concatenate(_split3(-cum_k), axis=1)
    k_aug = jnp.einsum('bjs,jc->bsc', ck3, jnp.asarray(place, MXU_DTYPE), preferred_element_type=F32) + ones_row
    k_aug = k_aug.astype(MXU_DTYPE).reshape(B, S, P, LANES)
    kk = jnp.concatenate([kv[..., :HD].reshape(B, S, P, LANES), k_aug], axis=-1).transpose(0, 2, 1, 3)
    qT = q.transpose(0, 2, 1).reshape(B, P, 2, HEAD_DIM, T)
    cq3 = [c.reshape(B, P, 2, 1, T) for c in _split3(cum_q)]
    one = jnp.ones((B, P, 3, T), MXU_DTYPE)
    zrows = lambda n: jnp.zeros((B, P, n, T), MXU_DTYPE)
    heads = []
    for e in range(2):
        rows = [qT[:, :, 0], zrows(HEAD_DIM)] if e == 0 else [zrows(HEAD_DIM), qT[:, :, 1]]
        rows += ([zrows(6 * e)] if e else []) + [one] + [c[:, :, e] for c in cq3] + [zrows(LANES - 6 * e - 6)]
        heads.append(jnp.concatenate(rows, axis=2))
    qt = jnp.stack(heads, axis=2).reshape(B, H, 2 * LANES, T)
    v = jnp.pad(kv[..., HD:], ((0, 0), (0, nc * tk - S), (0, 0)))
    vt = v.reshape(B, nc, tk, P, LANES).transpose(0, 3, 1, 4, 2)
    npp = (P if P <= 8 else 1) if nq == 1 else (2 if P % 2 == 0 else 1)
    ot = pl.pallas_call(
        functools.partial(_fox_body, tq=tq, tk=tk, q_off=q_off, nq=nq, npp=npp),
        grid=(B, P // npp, nq),
        in_specs=[pl.BlockSpec((1, 2 * npp, 2 * LANES, tq), lambda b, p, i: (b, p, 0, i)),
                  pl.BlockSpec((1, npp, S, 2 * LANES), lambda b, p, i: (b, p, 0, 0)),
                  pl.BlockSpec((1, npp, nc, LANES, tk), lambda b, p, i: (b, p, 0, 0, 0))],
        out_specs=pl.BlockSpec((1, npp, LANES, tq), lambda b, p, i: (b, p, 0, i)),
        out_shape=jax.ShapeDtypeStruct((B, P, LANES, T), MXU_DTYPE),
        compiler_params=_params("arbitrary", "arbitrary", "arbitrary"),
        name="fox_attention",
    )(qt, kk, vt)
    return ot.transpose(0, 3, 1, 2).reshape(B, T, HD)


def _dsa_body(qt_ref, k_ref, vt_ref, qit_ref, kid_ref, wit_ref, o_ref, keys_scr, bias_scr, j_scr, *,
              tq, tk, q_off, s_real, topk, n_heads, n_kv, n_idx, nq, n_chunks):
    if nq == 1:
        q0 = q_off
        n_ch = min((q0 + tq + tk - 1) // tk, n_chunks)
        unscanned = float(s_real - min(n_ch * tk, s_real))
    else:
        q0 = q_off + pl.program_id(1) * tq
        n_ch = jnp.minimum((q0 + tq + tk - 1) // tk, n_chunks)
        unscanned = (s_real - jnp.minimum(n_ch * tk, s_real)).astype(F32)
    krow = lax.broadcasted_iota(jnp.int32, (tk, tq), 0)
    qchunk = (q0 + lax.broadcasted_iota(jnp.int32, (tk, tq), 1)) // CHUNK

    def admissible(kpos):
        return ((kpos // CHUNK) <= qchunk) & (kpos < s_real)

    wit = wit_ref[0]

    def score_chunk(c, _):
        kc = kid_ref[0, pl.ds(pl.multiple_of(c * tk, tk), tk), :]
        score = jnp.zeros((tk, tq), F32)
        for h in range(n_idx):
            s = jnp.dot(kc, qit_ref[0, h], preferred_element_type=F32)
            score = score + wit[h:h + 1, :] * jnp.maximum(s, 0.0)
        kpos = c * tk + krow
        sm = jnp.where(admissible(kpos), score, NEG)
        bits = pltpu.bitcast(sm, jnp.int32)
        key = jnp.where(bits < 0, bits ^ 0x7FFFFFFF, bits)
        key = jnp.where(bits == INT_MIN, 0, key)
        keys_scr[c] = jnp.where(kpos < s_real, key, INT_MIN)
        return 0

    lax.fori_loop(0, n_ch, score_chunk, 0)

    if nq == 1:
        if n_ch % 2:
            keys_scr[n_ch] = jnp.full((tk, tq), INT_MIN, jnp.int32)
    else:
        @pl.when(n_ch % 2 == 1)
        def _():
            keys_scr[n_ch] = jnp.full((tk, tq), INT_MIN, jnp.int32)

    def count(pred):
        def body(i, acc):
            for c in (2 * i, 2 * i + 1):
                x = jnp.where(pred(keys_scr[c], c), 1.0, 0.0)
                acc = acc + jnp.sum(x.reshape(tk // 8, 8, tq), axis=0)
            return acc
        acc = lax.fori_loop(0, (n_ch + 1) // 2, body, jnp.zeros((8, tq), F32))
        return jnp.sum(acc, axis=0, keepdims=True)

    kf = float(topk)

    def bit_step(it, carry):
        t, cge = carry
        cand = t + jnp.left_shift(jnp.int32(1), 31 - it)
        cnt = count(lambda kc, c: kc >= cand) + jnp.where(cand <= KEY_NEG, unscanned, 0.0)
        ok = cnt >= kf
        return jnp.where(ok, cand, t), jnp.where(ok, cnt, cge)

    thr, cge = lax.fori_loop(0, 32, bit_step,
                             (jnp.full((1, tq), INT_MIN, jnp.int32), jnp.full((1, tq), 3e38, F32)))
    cgt = count(lambda kc, c: kc > thr) + jnp.where(thr < KEY_NEG, unscanned, 0.0)
    need = kf - cgt

    n_bits = int(n_chunks * tk).bit_length()
    j_scr[...] = jnp.full((1, tq), 2 ** n_bits, jnp.int32)

    @pl.when(jnp.max(cge) > kf)
    def _():
        def j_step(it, jcur):
            cand = jcur + jnp.left_shift(jnp.int32(1), n_bits - 1 - it)
            f = count(lambda kc, c: (kc == thr) & ((c * tk + krow) < cand))
            return jnp.where(f < need, cand, jcur)
        j_scr[...] = lax.fori_loop(0, n_bits, j_step, jnp.zeros((1, tq), jnp.int32))

    jmax = j_scr[...]

    def bias_chunk(c, _):
        kc = keys_scr[c]
        kpos = c * tk + krow
        sel = (kc > thr) | ((kc == thr) & (kpos <= jmax))
        bias_scr[c] = jnp.where(sel & admissible(kpos), 0.0, NEG)
        return 0

    lax.fori_loop(0, n_ch, bias_chunk, 0)

    G2 = 2 * n_heads // n_kv
    outs = []
    for r in range(n_kv // 2):

        def attend(c, carry, r=r):
            kc = k_ref[0, pl.ds(pl.multiple_of(c * tk, tk), tk), r * LANES:(r + 1) * LANES]
            vt = vt_ref[0, r, c]
            b = bias_scr[c]
            sts = [jnp.dot(kc, qt_ref[0, r * G2 + jh], preferred_element_type=F32) + b for jh in range(G2)]
            return _softmax_steps(sts, vt, carry)

        init = (jnp.full((1, tq), NEG, F32), jnp.zeros((1, tq), F32), jnp.zeros((LANES, tq), F32)) * G2
        carry = lax.fori_loop(0, n_ch, attend, init)
        for jh in range(G2):
            kh = (2 * jh) // G2
            outs.append((carry[3 * jh + 2] / carry[3 * jh + 1])[kh * HEAD_DIM:(kh + 1) * HEAD_DIM])
    o_ref[0] = jnp.concatenate(outs, axis=0).astype(o_ref.dtype)


def _dsa_attention(q, k, v, qi, kid, wi, *, s_real, topk, tq, tk):
    B, T, HD = q.shape
    s_pad = k.shape[1]
    n_heads, n_kv, n_idx = HD // HEAD_DIM, k.shape[2] // HEAD_DIM, qi.shape[2] // HEAD_DIM
    assert n_kv % 2 == 0 and n_heads % n_kv == 0 and s_pad % tk == 0 and T % tq == 0
    G = n_heads // n_kv
    nq = T // tq
    nc = s_pad // tk
    qh = q.transpose(0, 2, 1).reshape(B, n_heads, HEAD_DIM, T)
    upper = ((np.arange(n_heads) // G) % 2 == 1)[None, :, None, None]
    zq = jnp.zeros_like(qh)
    qt = jnp.concatenate([jnp.where(upper, zq, qh), jnp.where(upper, qh, zq)], axis=2)
    qih = qi.transpose(0, 2, 1).reshape(B, n_idx, HEAD_DIM, T)
    qit = jnp.concatenate([qih, jnp.zeros_like(qih)], axis=2)
    wit = wi.transpose(0, 2, 1)
    vt = v.reshape(B, nc, tk, n_kv // 2, LANES).transpose(0, 3, 1, 4, 2)
    body = functools.partial(_dsa_body, tq=tq, tk=tk, q_off=s_real - T, s_real=s_real, topk=topk,
                             n_heads=n_heads, n_kv=n_kv, n_idx=n_idx, nq=nq, n_chunks=nc)
    ot = pl.pallas_call(
        body,
        grid=(B, nq),
        in_specs=[pl.BlockSpec((1, n_heads, LANES, tq), lambda b, i: (b, 0, 0, i)),
                  pl.BlockSpec((1, s_pad, k.shape[2]), lambda b, i: (b, 0, 0)),
                  pl.BlockSpec((1, n_kv // 2, nc, LANES, tk), lambda b, i: (b, 0, 0, 0, 0)),
                  pl.BlockSpec((1, n_idx, LANES, tq), lambda b, i: (b, 0, 0, i)),
                  pl.BlockSpec((1, s_pad, LANES), lambda b, i: (b, 0, 0)),
                  pl.BlockSpec((1, n_idx, tq), lambda b, i: (b, 0, i))],
        out_specs=pl.BlockSpec((1, HD, tq), lambda b, i: (b, 0, i)),
        out_shape=jax.ShapeDtypeStruct((B, HD, T), MXU_DTYPE),
        scratch_shapes=[pltpu.VMEM((nc + 1, tk, tq), jnp.int32),
                        pltpu.VMEM((nc, tk, tq), F32),
                        pltpu.VMEM((1, tq), jnp.int32)],
        compiler_params=_params("arbitrary", "arbitrary"),
        name="dsa_attention",
    )(qt, k, vt, qit, kid, wit)
    return ot.transpose(0, 2, 1)


def _swa_body(sinks_ref, q_ref, k0_ref, k1_ref, k2_ref, v0_ref, v1_ref, v2_ref, o_ref, *,
              first_valid, n_heads, n_kv):
    c = pl.program_id(1)
    q = q_ref[0]
    k = jnp.concatenate([k0_ref[0], k1_ref[0], k2_ref[0]], axis=0)
    v = jnp.concatenate([v0_ref[0], v1_ref[0], v2_ref[0]], axis=0)
    n_keys = k.shape[0]
    G = n_heads // n_kv
    valid = (c * CHUNK + lax.broadcasted_iota(jnp.int32, (G * CHUNK, n_keys), 1)) >= first_valid
    ss = []
    for g in range(n_kv):
        qs = jnp.concatenate([q[:, h * HEAD_DIM:(h + 1) * HEAD_DIM] for h in range(g * G, (g + 1) * G)], axis=0)
        ss.append(lax.dot_general(qs, k[:, g * HEAD_DIM:(g + 1) * HEAD_DIM], _NT, preferred_element_type=F32))
    ps = []
    for g in range(n_kv):
        sink = jnp.concatenate([jnp.full((CHUNK, 1), sinks_ref[h], F32) for h in range(g * G, (g + 1) * G)], axis=0)
        s = jnp.where(valid, ss[g], NEG)
        m = jnp.maximum(jnp.max(s, axis=1, keepdims=True), sink)
        e = jnp.exp(s - m)
        ps.append((e / (jnp.sum(e, axis=1, keepdims=True) + jnp.exp(sink - m))).astype(MXU_DTYPE))
    pieces = []
    for g in range(n_kv):
        o = jnp.dot(ps[g], v[:, g * HEAD_DIM:(g + 1) * HEAD_DIM], preferred_element_type=F32)
        pieces += [o[jh * CHUNK:(jh + 1) * CHUNK] for jh in range(G)]
    o_ref[0] = jnp.concatenate(pieces, axis=1).astype(o_ref.dtype)


def _swa_attention(q, kpad, vpad, sinks, *, first_valid):
    B, T, HD = q.shape
    KD = kpad.shape[2]
    n_heads, n_kv = HD // HEAD_DIM, KD // HEAD_DIM
    kv_specs = [pl.BlockSpec((1, CHUNK, KD), functools.partial(lambda b, c, s, o: (b, c + o, 0), o=o))
                for o in range(3)]
    return pl.pallas_call(
        functools.partial(_swa_body, first_valid=first_valid, n_heads=n_heads, n_kv=n_kv),
        grid_spec=pltpu.PrefetchScalarGridSpec(
            num_scalar_prefetch=1,
            grid=(B, T // CHUNK),
            in_specs=[pl.BlockSpec((1, CHUNK, HD), lambda b, c, s: (b, c, 0))] + kv_specs + kv_specs,
            out_specs=pl.BlockSpec((1, CHUNK, HD), lambda b, c, s: (b, c, 0))),
        out_shape=jax.ShapeDtypeStruct((B, T, HD), MXU_DTYPE),
        compiler_params=_params("arbitrary", "arbitrary"),
        name="swa_attention",
    )(sinks.astype(F32), q, kpad, kpad, kpad, vpad, vpad, vpad)


def _gmm1_body(be_ref, nu_ref, x_ref, wg_ref, wu_ref, bg_ref, bu_ref, o_ref, wg_s, wu_s):
    m = pl.program_id(1)
    changed = (m == 0) | (be_ref[m] != be_ref[jnp.maximum(m - 1, 0)])

    @pl.when(changed)
    def _():
        wg_s[...] = wg_ref[0, 0].astype(MXU_DTYPE)
        wu_s[...] = wu_ref[0, 0].astype(MXU_DTYPE)

    def compute(nrows):
        x = x_ref[0:nrows, :]
        x = _unpack_pairs(x) if x.dtype == jnp.int32 else x.astype(MXU_DTYPE)
        tf = wg_s.shape[1]
        sb = min(tf, 2 * LANES)
        bg, bu = bg_ref[0, 0], bu_ref[0, 0]
        for c in range(tf // sb):
            sl = slice(c * sb, (c + 1) * sb)
            g = jnp.dot(x, wg_s[:, sl], preferred_element_type=F32) + bg[:, sl]
            u = jnp.dot(x, wu_s[:, sl], preferred_element_type=F32) + bu[:, sl]
            g = jnp.minimum(g, SWIGLU_LIMIT)
            u = jnp.clip(u, -SWIGLU_LIMIT, SWIGLU_LIMIT)
            o_ref[0:nrows, sl] = (g * _sigmoid(SWIGLU_ALPHA * g) * (u + 1.0)).astype(o_ref.dtype)

    _row_block_dispatch(m, nu_ref, o_ref, compute)


def _row_block_dispatch(m, nu_ref, o_ref, compute):
    tm = o_ref.shape[0]
    half = tm // 2
    used = m < nu_ref[0]
    if tm >= 2 * LANES:
        @pl.when(used & (nu_ref[1 + m] > half))
        def _():
            compute(tm)

        @pl.when(used & (nu_ref[1 + m] <= half))
        def _():
            compute(half)
            o_ref[half:, :] = jnp.zeros((tm - half, o_ref.shape[1]), o_ref.dtype)
    else:
        @pl.when(used)
        def _():
            compute(tm)

    @pl.when(jnp.logical_not(used))
    def _():
        o_ref[...] = jnp.zeros(o_ref.shape, o_ref.dtype)


def _gmm2_body(be_ref, nu_ref, a_ref, w_ref, b_ref, o_ref, w_s):
    m = pl.program_id(1)
    changed = (m == 0) | (be_ref[m] != be_ref[jnp.maximum(m - 1, 0)])

    @pl.when(changed)
    def _():
        w_s[...] = w_ref[0, 0].astype(MXU_DTYPE)

    def compute(nrows):
        o_ref[0:nrows, :] = jnp.dot(a_ref[0:nrows, :], w_s[...], preferred_element_type=F32) + b_ref[0, 0]

    _row_block_dispatch(m, nu_ref, o_ref, compute)


def _moe_experts(xs, blk_e, n_used, layer, w_gu, b_gu, w_dn, b_dn, *, tm, tf=512, tn=1024):
    R, xw = xs.shape
    L, E, D, F2 = w_gu.shape
    Fh = F2 // 2
    tf = _tile(Fh, tf)
    tn = _tile(D, tn)
    nb = R // tm
    nf = Fh // tf

    def xmap(j, m, be, nu):
        return (jnp.minimum(m, nu[0] - 1), 0)

    act = pl.pallas_call(
        _gmm1_body,
        grid_spec=pltpu.PrefetchScalarGridSpec(
            num_scalar_prefetch=2,
            grid=(nf, nb),
            in_specs=[pl.BlockSpec((tm, xw), xmap),
                      pl.BlockSpec((1, 1, D, tf), lambda j, m, be, nu: (layer, be[m], 0, j)),
                      pl.BlockSpec((1, 1, D, tf), lambda j, m, be, nu: (layer, be[m], 0, nf + j)),
                      pl.BlockSpec((1, 1, 1, tf), lambda j, m, be, nu: (layer, be[m], 0, j)),
                      pl.BlockSpec((1, 1, 1, tf), lambda j, m, be, nu: (layer, be[m], 0, nf + j))],
            out_specs=pl.BlockSpec((tm, tf), lambda j, m, be, nu: (m, j)),
            scratch_shapes=[pltpu.VMEM((D, tf), MXU_DTYPE), pltpu.VMEM((D, tf), MXU_DTYPE)]),
        out_shape=jax.ShapeDtypeStruct((R, Fh), MXU_DTYPE),
        compiler_params=_params("arbitrary", "arbitrary"),
        name="moe_gate_up",
    )(blk_e, n_used, xs, w_gu, w_gu, b_gu.reshape(L, E, 1, F2), b_gu.reshape(L, E, 1, F2))

    ys = pl.pallas_call(
        _gmm2_body,
        grid_spec=pltpu.PrefetchScalarGridSpec(
            num_scalar_prefetch=2,
            grid=(D // tn, nb),
            in_specs=[pl.BlockSpec((tm, Fh), xmap),
                      pl.BlockSpec((1, 1, Fh, tn), lambda j, m, be, nu: (layer, be[m], 0, j)),
                      pl.BlockSpec((1, 1, 1, tn), lambda j, m, be, nu: (layer, be[m], 0, j))],
            out_specs=pl.BlockSpec((tm, tn), lambda j, m, be, nu: (m, j)),
            scratch_shapes=[pltpu.VMEM((Fh, tn), MXU_DTYPE)]),
        out_shape=jax.ShapeDtypeStruct((R, D), F32),
        compiler_params=_params("arbitrary", "arbitrary"),
        name="moe_down",
    )(blk_e, n_used, act, w_dn, b_dn.reshape(L, E, 1, D))
    return ys


def _combine_body(y_ref, gate_ref, res_ref, mg_ref, o_ref):
    gate = gate_ref[...]
    y = None
    for k in range(y_ref.shape[0]):
        t = gate[:, k:k + 1] * y_ref[k]
        y = t if y is None else y + t
    tm, d = y.shape
    o_ref[...] = res_ref[...] + mg_ref[...] * y.reshape(tm // CHUNK, CHUNK, d)


def _moe_combine(yk, gate, res3, mod_exp, k_gate, *, tm=256):
    K, N, D = yk.shape
    NB, C, _ = res3.shape
    tm = _tile(N, tm, C)
    return pl.pallas_call(
        _combine_body,
        grid=(N // tm,),
        in_specs=[pl.BlockSpec((K, tm, D), lambda i: (0, i, 0)),
                  pl.BlockSpec((tm, K), lambda i: (i, 0)),
                  pl.BlockSpec((tm // C, C, D), lambda i: (i, 0, 0)),
                  pl.BlockSpec((tm // C, 1, D), lambda i: (i, 0, k_gate))],
        out_specs=pl.BlockSpec((tm // C, C, D), lambda i: (i, 0, 0)),
        out_shape=jax.ShapeDtypeStruct((NB, C, D), F32),
        compiler_params=_params("arbitrary"),
        name="moe_combine",
    )(yk, gate, res3, mod_exp)


def _moe(h, hp, layer, res3, mod_exp, w_router, b_router, w_gu, b_gu, w_dn, b_dn, *, tm):
    N, D = h.shape
    E = w_router.shape[-1]
    wr = jnp.pad(w_router[layer], ((0, 0), (0, LANES - E)))
    br = jnp.pad(b_router[layer], (0, LANES - E)).reshape(1, LANES)
    logits = _mm(h, wr, [F32], tn=LANES, bias=br)[0][:, :E]
    top_logit, top_e = lax.top_k(logits, TOP_K)
    gate = jax.nn.softmax(top_logit, axis=-1)
    e_flat = top_e.reshape(-1)
    onehot = (e_flat[:, None] == jnp.arange(E, dtype=jnp.int32)[None, :]).astype(jnp.int32)
    rank = jnp.take_along_axis(jnp.cumsum(onehot, axis=0) - onehot, e_flat[:, None], axis=1)[:, 0]
    counts = jnp.sum(onehot, axis=0)
    padded = (counts + tm - 1) // tm * tm
    pend = jnp.cumsum(padded)
    dest = (pend - padded)[e_flat] + rank
    n_blocks = -(-N * TOP_K // tm) + E
    src_tok = jnp.zeros((n_blocks * tm,), jnp.int32).at[dest].set(jnp.arange(N * TOP_K, dtype=jnp.int32) // TOP_K)
    blk_start = jnp.arange(n_blocks, dtype=jnp.int32) * tm
    blk_e = jnp.minimum(jnp.sum((pend[None, :] <= blk_start[:, None]).astype(jnp.int32), axis=1), E - 1)
    rows_valid = jnp.clip((pend - padded + counts)[blk_e] - blk_start, 0, tm)
    n_used = jnp.concatenate([(pend[-1] // tm).reshape(1), rows_valid]).astype(jnp.int32)
    xs = hp[src_tok]
    ys = _moe_experts(xs, blk_e, n_used, layer, w_gu, b_gu, w_dn, b_dn, tm=tm)
    yk = ys[dest.reshape(N, TOP_K).T.reshape(-1)].reshape(TOP_K, N, D)
    return _moe_combine(yk, gate, res3, mod_exp, 5)


def kernel(x_prompt, x_sample, c_prompt, c_sample, cache_fox_kv, cache_fox_logf, cache_dsa_kv, cache_dsa_kidx, cache_swa_kv, w_ada, b_ada, g_mix, g_ffn, w_in_ab, b_forget, w_out_ab, w_in_c, sinks_c, w_out_c, w_router, b_router, w_gu, b_gu, w_dn, b_dn, g_final):
    Bp, Tp, D = x_prompt.shape
    Bs, Ts, _ = x_sample.shape
    depth = w_ada.shape[0]
    past_len = cache_fox_kv.shape[2]
    h_fox = cache_fox_kv.shape[4]
    kv_dsa = cache_dsa_kv.shape[4]
    d_idx = cache_dsa_kidx.shape[3]
    window = cache_swa_kv.shape[2]
    kv_swa = cache_swa_kv.shape[4]
    h_swa = sinks_c.shape[1]
    h_dsa = w_out_ab.shape[1] // HEAD_DIM - h_fox
    fox_w, dq_w, dkv_w = h_fox * HEAD_DIM, h_dsa * HEAD_DIM, kv_dsa * HEAD_DIM
    h_idx = (w_in_ab.shape[2] - 3 * fox_w - h_fox - dq_w - 2 * dkv_w - d_idx) // (d_idx + 1)
    assert Bp == 1 and Ts == CHUNK and Tp % CHUNK == 0 and past_len % CHUNK == 0
    assert d_idx == HEAD_DIM and window == 2 * CHUNK and d_idx + h_fox + h_idx <= LANES
    Mp, Ms = Bp * Tp, Bs * Ts
    M = Mp + Ms
    NB = M // CHUNK
    hd_scale = HEAD_DIM ** -0.5

    pos = jnp.concatenate([jnp.arange(Tp, dtype=jnp.int32),
                           jnp.tile(past_len + jnp.arange(Ts, dtype=jnp.int32), Bs)])
    half = HEAD_DIM // 2
    inv = ROPE_THETA ** (-jnp.arange(half, dtype=F32) / half)
    ang = pos.astype(F32)[:, None] * inv
    cos, sin = jnp.cos(ang), jnp.sin(ang)
    rope = (jnp.concatenate([cos, cos, cos, cos], axis=1), jnp.concatenate([-sin, sin, -sin, sin], axis=1))

    n_c = Bp + Bs
    c_all = jnp.pad(jnp.concatenate([c_prompt, c_sample], axis=0), ((0, -n_c % 8), (0, 0)))
    mod = _ada(c_all, w_ada, b_ada)
    blk_row = jnp.concatenate([jnp.zeros((Mp // CHUNK,), jnp.int32),
                               Bp + jnp.arange(Bs, dtype=jnp.int32)])
    x3 = jnp.concatenate([x_prompt.reshape(Mp // CHUNK, CHUNK, D), x_sample.reshape(Ms // CHUNK, CHUNK, D)], axis=0)

    fox_kv_p, fox_kv_s, logf_p, logf_s, dsa_kv_p, dsa_kv_s, kidx_p, kidx_s, swa_p, swa_s = ([] for _ in range(10))
    moe_tm = 512 if M * TOP_K >= 16384 else 64

    for l in range(depth):
        mod_exp = mod[l][blk_row][:, None, :]
        h = _norm_mod(x3, g_mix[l], mod_exp, 0, 1).reshape(M, D)
        j = l // 2
        if l % 2 == 0:
            w = w_in_ab[j]
            offs = np.cumsum([0, fox_w, fox_w, fox_w, h_fox, dq_w, dkv_w, dkv_w, h_idx * d_idx, d_idx, h_idx])
            seg = lambda a, b: w[:, offs[a]:offs[b]].astype(MXU_DTYPE)
            q_fox = _mm(h, seg(0, 1), [MXU_DTYPE], scale=hd_scale)[0]
            fkv32, fkv = _mm(h, seg(1, 3), [F32, MXU_DTYPE])
            q_dsa = _mm(h, seg(4, 5), [MXU_DTYPE], scale=hd_scale, rope=rope, rope_tiles=10 ** 6)[0]
            dkv32 = _mm(h, seg(5, 7), [F32], tn=dkv_w, rope=rope, rope_tiles=1)[0]
            qi = _mm(h, seg(7, 8), [MXU_DTYPE], scale=d_idx ** -0.5, rope=rope, rope_tiles=10 ** 6)[0]
            pad = LANES - d_idx - h_fox - h_idx
            w_small = jnp.concatenate([w[:, offs[8]:offs[9]], w[:, offs[3]:offs[4]], w[:, offs[9]:offs[10]],
                                       jnp.zeros((D, pad), w.dtype)], axis=1).astype(MXU_DTYPE)
            bf_row = jnp.pad(b_forget[j], (d_idx, LANES - d_idx - h_fox)).reshape(1, LANES)
            small = _mm_small(h, w_small, rope, bf_row, d_idx=d_idx, h_fox=h_fox, wi_scale=h_idx ** -0.5)
            kidx, logf, wi = small[:, :d_idx], small[:, d_idx:d_idx + h_fox], small[:, d_idx + h_fox:d_idx + h_fox + h_idx]

            fox_kv_p.append(fkv32[:Mp].reshape(Bp, Tp, 2, h_fox, HEAD_DIM))
            fox_kv_s.append(fkv32[Mp:].reshape(Bs, Ts, 2, h_fox, HEAD_DIM))
            logf_p.append(logf[:Mp].reshape(Bp, Tp, h_fox))
            logf_s.append(logf[Mp:].reshape(Bs, Ts, h_fox))
            dsa_kv_p.append(dkv32[:Mp].reshape(Bp, Tp, 2, kv_dsa, HEAD_DIM))
            dsa_kv_s.append(dkv32[Mp:].reshape(Bs, Ts, 2, kv_dsa, HEAD_DIM))
            kidx_p.append(kidx[:Mp].reshape(Bp, Tp, d_idx))
            kidx_s.append(kidx[Mp:].reshape(Bs, Ts, d_idx))

            cum_p = jnp.cumsum(logf_p[-1].transpose(0, 2, 1), axis=2)
            oa_p = _fox_attention(q_fox[:Mp].reshape(Bp, Tp, fox_w), fkv[:Mp].reshape(Bp, Tp, 2 * fox_w),
                                  cum_p, cum_p, tq=min(512, Tp), tk=min(512, Tp))
            kv_s = jnp.concatenate([cache_fox_kv[j].reshape(Bs, past_len, 2 * fox_w).astype(MXU_DTYPE),
                                    fkv[Mp:].reshape(Bs, Ts, 2 * fox_w)], axis=1)
            cum_s = jnp.cumsum(jnp.concatenate([cache_fox_logf[j].astype(F32), logf_s[-1]], axis=1).transpose(0, 2, 1),
                               axis=2)
            oa_s = _fox_attention(q_fox[Mp:].reshape(Bs, Ts, fox_w), kv_s, cum_s[:, :, past_len:], cum_s,
                                  tq=Ts, tk=_tile(past_len, 256, CHUNK))
            dkv = dkv32.astype(MXU_DTYPE)
            kid = jnp.concatenate([kidx, kidx], axis=1).astype(MXU_DTYPE)
            ob_p = _dsa_attention(q_dsa[:Mp].reshape(Bp, Tp, dq_w), dkv[:Mp, :dkv_w].reshape(Bp, Tp, dkv_w),
                                  dkv[:Mp, dkv_w:].reshape(Bp, Tp, dkv_w), qi[:Mp].reshape(Bp, Tp, -1),
                                  kid[:Mp].reshape(Bp, Tp, LANES), wi[:Mp].reshape(Bp, Tp, h_idx),
                                  s_real=Tp, topk=min(TOPK_MAX, Tp // 4), tq=min(256, Tp), tk=min(512, Tp))
            S = past_len + Ts
            s_pad = -(-S // LANES) * LANES
            padk = lambda a: jnp.pad(a, ((0, 0), (0, s_pad - S), (0, 0)))
            pkv = cache_dsa_kv[j].reshape(Bs, past_len, 2 * dkv_w).astype(MXU_DTYPE)
            k_s = padk(jnp.concatenate([pkv[:, :, :dkv_w], dkv[Mp:, :dkv_w].reshape(Bs, Ts, dkv_w)], axis=1))
            v_s = padk(jnp.concatenate([pkv[:, :, dkv_w:], dkv[Mp:, dkv_w:].reshape(Bs, Ts, dkv_w)], axis=1))
            pki = cache_dsa_kidx[j].astype(MXU_DTYPE)
            kid_s = padk(jnp.concatenate([jnp.concatenate([pki, pki], axis=2), kid[Mp:].reshape(Bs, Ts, LANES)], axis=1))
            ob_s = _dsa_attention(q_dsa[Mp:].reshape(Bs, Ts, dq_w), k_s, v_s, qi[Mp:].reshape(Bs, Ts, -1),
                                  kid_s, wi[Mp:].reshape(Bs, Ts, h_idx),
                                  s_real=S, topk=min(TOPK_MAX, S // 4), tq=Ts, tk=_tile(s_pad, 512))
            oa = jnp.concatenate([oa_p.reshape(Mp, fox_w), oa_s.reshape(Ms, fox_w)], axis=0)
            ob = jnp.concatenate([ob_p.reshape(Mp, dq_w), ob_s.reshape(Ms, dq_w)], axis=0)
            if fox_w == dq_w:
                x3 = _mm_res([oa, ob], w_out_ab, j, x3, mod_exp, 2)
            else:
                x3 = _mm_res([jnp.concatenate([oa, ob], axis=1)], w_out_ab, j, x3, mod_exp, 2)
        else:
            w = w_in_c[j]
            qw, kw = h_swa * HEAD_DIM, kv_swa * HEAD_DIM
            q = _mm(h, w[:, :qw].astype(MXU_DTYPE), [MXU_DTYPE], scale=hd_scale, rope=rope, rope_tiles=10 ** 6)[0]
            kv32 = _mm(h, w[:, qw:].astype(MXU_DTYPE), [F32], tn=kw, rope=rope, rope_tiles=1)[0]
            kvb = kv32.astype(MXU_DTYPE)
            kv_new_p = kv32[:Mp].reshape(Bp, Tp, 2, kv_swa, HEAD_DIM)
            kv_new_s = kv32[Mp:].reshape(Bs, Ts, 2, kv_swa, HEAD_DIM)
            swa_p.append(kv_new_p[:, Tp - min(window, Tp):])
            swa_s.append(jnp.concatenate([cache_swa_kv[j], kv_new_s], axis=1)[:, Ts:])
            zpad = jnp.zeros((Bp, window, kw), MXU_DTYPE)
            o_p = _swa_attention(q[:Mp].reshape(Bp, Tp, qw),
                                 jnp.concatenate([zpad, kvb[:Mp, :kw].reshape(Bp, Tp, kw)], axis=1),
                                 jnp.concatenate([zpad, kvb[:Mp, kw:].reshape(Bp, Tp, kw)], axis=1),
                                 sinks_c[j], first_valid=window)
            past = cache_swa_kv[j].reshape(Bs, window, 2 * kw).astype(MXU_DTYPE)
            o_s = _swa_attention(q[Mp:].reshape(Bs, Ts, qw),
                                 jnp.concatenate([past[:, :, :kw], kvb[Mp:, :kw].reshape(Bs, Ts, kw)], axis=1),
                                 jnp.concatenate([past[:, :, kw:], kvb[Mp:, kw:].reshape(Bs, Ts, kw)], axis=1),
                                 sinks_c[j], first_valid=0)
            o = jnp.concatenate([o_p.reshape(Mp, qw), o_s.reshape(Ms, qw)], axis=0)
            x3 = _mm_res([o], w_out_c, j, x3, mod_exp, 2)

        h2, h2p = _norm_mod(x3, g_ffn[l], mod_exp, 3, 4, packed=True)
        x3 = _moe(h2.reshape(M, D), h2p.reshape(M, -1), l, x3, mod_exp,
                  w_router, b_router, w_gu, b_gu, w_dn, b_dn, tm=moe_tm)

    y = _final_norm(x3, g_final).reshape(M, D)
    st = lambda xs: jnp.stack(xs, axis=0)
    return (y[:Mp].reshape(Bp, Tp, D), y[Mp:].reshape(Bs, Ts, D),
            st(fox_kv_p), st(fox_kv_s), st(logf_p), st(logf_s), st(dsa_kv_p), st(dsa_kv_s),
            st(kidx_p), st(kidx_s), st(swa_p), st(swa_s))
```

```python
import functools

import numpy as np
import jax
import jax.numpy as jnp
from jax import lax
from jax.experimental import pallas as pl
from jax.experimental.pallas import tpu as pltpu

CHUNK = 64
HEAD_DIM = 64
ROPE_THETA = 10000.0
EPS = 1e-5
NEG = -1e30
TOPK_MAX = 256
TOP_K = 4
SWIGLU_ALPHA = 1.702
SWIGLU_LIMIT = 7.0

LANES = 128
F32 = jnp.float32
MXU_DTYPE = jnp.bfloat16
VMEM_LIMIT = 56 * 1024 * 1024

INT_MIN = -2 ** 31
_negbits = int(np.array(NEG, np.float32).view(np.int32))
KEY_NEG = _negbits ^ 0x7FFFFFFF

_NT = (((1,), (1,)), ((), ()))


def _params(*sem):
    return pltpu.CompilerParams(dimension_semantics=sem, vmem_limit_bytes=VMEM_LIMIT)


def _tile(n, pref, mult=LANES):
    if n <= pref:
        return n
    t = pref - pref % mult
    while t >= mult:
        if n % t == 0:
            return t
        t -= mult
    return n


def _sigmoid(x):
    return 1.0 / (1.0 + jnp.exp(-x))


def _ada_body(c_ref, w_ref, b_ref, o_ref):
    c = c_ref[...]
    a = (c * _sigmoid(c)).astype(MXU_DTYPE)
    o_ref[0] = jnp.dot(a, w_ref[0].astype(MXU_DTYPE), preferred_element_type=F32) + b_ref[0]


def _ada(c_all, w_ada, b_ada):
    L, D, N = w_ada.shape
    R = c_all.shape[0]
    tn = _tile(N, 1024)
    return pl.pallas_call(
        _ada_body,
        grid=(L, N // tn),
        in_specs=[pl.BlockSpec((R, D), lambda l, j: (0, 0)),
                  pl.BlockSpec((1, D, tn), lambda l, j: (l, 0, j)),
                  pl.BlockSpec((1, 1, tn), lambda l, j: (l, 0, j))],
        out_specs=pl.BlockSpec((1, R, tn), lambda l, j: (l, 0, j)),
        out_shape=jax.ShapeDtypeStruct((L, R, N), F32),
        compiler_params=_params("arbitrary", "arbitrary"),
        name="ada_mod",
    )(c_all, w_ada, b_ada.reshape(L, 1, N))


def _norm_mod_body(x_ref, g_ref, sh_ref, sc_ref, o_ref):
    x = x_ref[...]
    ms = jnp.mean(x * x, axis=-1, keepdims=True)
    xn = x * lax.rsqrt(ms + EPS) * g_ref[...]
    o_ref[...] = (xn * (1.0 + sc_ref[...]) + sh_ref[...]).astype(o_ref.dtype)


def _norm_body(x_ref, g_ref, o_ref):
    x = x_ref[...]
    ms = jnp.mean(x * x, axis=-1, keepdims=True)
    o_ref[...] = (x * lax.rsqrt(ms + EPS) * g_ref[...]).astype(o_ref.dtype)


def _pack_pairs(h):
    half = h.shape[-1] // 2
    bits = pltpu.bitcast(h.astype(jnp.bfloat16).astype(F32), jnp.int32)
    return bits[..., half:] | lax.shift_right_logical(bits[..., :half], 16)


def _unpack_pairs(w):
    lo = pltpu.bitcast(lax.shift_left(w, 16), F32).astype(jnp.bfloat16)
    hi = pltpu.bitcast(w & jnp.int32(-65536), F32).astype(jnp.bfloat16)
    return jnp.concatenate([lo, hi], axis=1)


def _norm_mod_pack_body(x_ref, g_ref, sh_ref, sc_ref, o_ref, p_ref):
    x = x_ref[...]
    ms = jnp.mean(x * x, axis=-1, keepdims=True)
    xn = x * lax.rsqrt(ms + EPS) * g_ref[...]
    h = xn * (1.0 + sc_ref[...]) + sh_ref[...]
    o_ref[...] = h.astype(o_ref.dtype)
    p_ref[...] = _pack_pairs(h) if p_ref.dtype == jnp.int32 else h


def _norm_mod(x3, g, mod_exp, k_shift, k_scale, packed=False):
    NB, C, D = x3.shape
    bb = _tile(NB, 8, 1)
    blk = pl.BlockSpec((bb, C, D), lambda i: (i, 0, 0))
    out_specs, out_shape, body = blk, jax.ShapeDtypeStruct((NB, C, D), MXU_DTYPE), _norm_mod_body
    if packed:
        body = _norm_mod_pack_body
        if MXU_DTYPE == jnp.bfloat16:
            out_specs = [blk, pl.BlockSpec((bb, C, D // 2), lambda i: (i, 0, 0))]
            out_shape = [out_shape, jax.ShapeDtypeStruct((NB, C, D // 2), jnp.int32)]
        else:
            out_specs = [blk, blk]
            out_shape = [out_shape, jax.ShapeDtypeStruct((NB, C, D), F32)]
    return pl.pallas_call(
        body,
        grid=(NB // bb,),
        in_specs=[blk,
                  pl.BlockSpec((1, 1, D), lambda i: (0, 0, 0)),
                  pl.BlockSpec((bb, 1, D), lambda i: (i, 0, k_shift)),
                  pl.BlockSpec((bb, 1, D), lambda i: (i, 0, k_scale))],
        out_specs=out_specs,
        out_shape=out_shape,
        compiler_params=_params("arbitrary"),
        name="norm_mod",
    )(x3, g.reshape(1, 1, D), mod_exp, mod_exp)


def _final_norm(x3, g):
    NB, C, D = x3.shape
    bb = _tile(NB, 8, 1)
    return pl.pallas_call(
        _norm_body,
        grid=(NB // bb,),
        in_specs=[pl.BlockSpec((bb, C, D), lambda i: (i, 0, 0)),
                  pl.BlockSpec((1, 1, D), lambda i: (0, 0, 0))],
        out_specs=pl.BlockSpec((bb, C, D), lambda i: (i, 0, 0)),
        out_shape=jax.ShapeDtypeStruct((NB, C, D), F32),
        compiler_params=_params("arbitrary"),
        name="final_norm",
    )(x3, g.reshape(1, 1, D))


def _rope_tile(acc, cos, sin):
    tn = acc.shape[1]
    reps = tn // LANES
    cosf = jnp.concatenate([cos] * reps, axis=1) if reps > 1 else cos
    sinf = jnp.concatenate([sin] * reps, axis=1) if reps > 1 else sin
    lane = lax.broadcasted_iota(jnp.int32, acc.shape, 1)
    first_half = (lane & (HEAD_DIM - 1)) < HEAD_DIM // 2
    rot = jnp.where(first_half, pltpu.roll(acc, tn - HEAD_DIM // 2, 1), pltpu.roll(acc, HEAD_DIM // 2, 1))
    return acc * cosf + rot * sinf


def _mm_body(*refs, scale, rope_tiles, n_col_tiles, has_rope, has_bias):
    x_ref, w_ref = refs[:2]
    idx = 2
    if has_rope:
        cos_ref, sin_ref = refs[idx:idx + 2]
        idx += 2
    if has_bias:
        b_ref = refs[idx]
        idx += 1
    out_refs = refs[idx:]
    acc = jnp.dot(x_ref[...], w_ref[...].astype(MXU_DTYPE), preferred_element_type=F32)
    if has_bias:
        acc = acc + b_ref[...]

    def emit(val):
        if scale != 1.0:
            val = val * scale
        for o in out_refs:
            o[...] = val.astype(o.dtype)

    if has_rope and rope_tiles > 0:
        if rope_tiles >= n_col_tiles:
            emit(_rope_tile(acc, cos_ref[...], sin_ref[...]))
        else:
            j = pl.program_id(1)

            @pl.when(j < rope_tiles)
            def _():
                emit(_rope_tile(acc, cos_ref[...], sin_ref[...]))

            @pl.when(j >= rope_tiles)
            def _():
                emit(acc)
    else:
        emit(acc)


def _mm(x, w, out_dtypes, *, tm=1024, tn=512, scale=1.0, rope=None, rope_tiles=0, bias=None):
    M, K = x.shape
    N = w.shape[1]
    tm = _tile(M, tm, 64)
    tn = _tile(N, tn)
    n_col = N // tn
    in_specs = [pl.BlockSpec((tm, K), lambda i, j: (i, 0)),
                pl.BlockSpec((K, tn), lambda i, j: (0, j))]
    args = [x, w]
    if rope is not None:
        in_specs += [pl.BlockSpec((tm, LANES), lambda i, j: (i, 0))] * 2
        args += list(rope)
    if bias is not None:
        in_specs.append(pl.BlockSpec((1, tn), lambda i, j: (0, j)))
        args.append(bias)
    outs = pl.pallas_call(
        functools.partial(_mm_body, scale=scale, rope_tiles=rope_tiles if rope is not None else 0,
                          n_col_tiles=n_col, has_rope=rope is not None, has_bias=bias is not None),
        grid=(M // tm, n_col),
        in_specs=in_specs,
        out_specs=[pl.BlockSpec((tm, tn), lambda i, j: (i, j)) for _ in out_dtypes],
        out_shape=[jax.ShapeDtypeStruct((M, N), dt) for dt in out_dtypes],
        compiler_params=_params("arbitrary", "arbitrary"),
        name="proj",
    )(*args)
    return outs


def _small_body(x_ref, w_ref, cos_ref, sin_ref, bf_ref, o_ref, *, d_idx, h_fox, wi_scale):
    acc = jnp.dot(x_ref[...], w_ref[...].astype(MXU_DTYPE), preferred_element_type=F32)
    roped = _rope_tile(acc, cos_ref[...], sin_ref[...])
    z = acc + bf_ref[...]
    logf = jnp.minimum(z, 0.0) - jnp.log1p(jnp.exp(-jnp.abs(z)))
    lane = lax.broadcasted_iota(jnp.int32, acc.shape, 1)
    o_ref[...] = jnp.where(lane < d_idx, roped, jnp.where(lane < d_idx + h_fox, logf, acc * wi_scale))


def _mm_small(x, w, rope, bf_row, *, d_idx, h_fox, wi_scale, tm=1024):
    M, K = x.shape
    tm = _tile(M, tm, 64)
    return pl.pallas_call(
        functools.partial(_small_body, d_idx=d_idx, h_fox=h_fox, wi_scale=wi_scale),
        grid=(M // tm,),
        in_specs=[pl.BlockSpec((tm, K), lambda i: (i, 0)),
                  pl.BlockSpec((K, LANES), lambda i: (0, 0)),
                  pl.BlockSpec((tm, LANES), lambda i: (i, 0)),
                  pl.BlockSpec((tm, LANES), lambda i: (i, 0)),
                  pl.BlockSpec((1, LANES), lambda i: (0, 0))],
        out_specs=pl.BlockSpec((tm, LANES), lambda i: (i, 0)),
        out_shape=jax.ShapeDtypeStruct((M, LANES), F32),
        compiler_params=_params("arbitrary"),
        name="proj_small",
    )(x, w, rope[0], rope[1], bf_row)


def _mm_res_body(*refs, n_pairs):
    xs = refs[:n_pairs]
    ws = refs[n_pairs:2 * n_pairs]
    res_ref, gate_ref, o_ref = refs[2 * n_pairs:]
    acc = None
    for x_ref, w_ref in zip(xs, ws):
        part = jnp.dot(x_ref[...], w_ref[0].astype(MXU_DTYPE), preferred_element_type=F32)
        acc = part if acc is None else acc + part
    tm, tn = acc.shape
    o_ref[...] = res_ref[...] + gate_ref[...] * acc.reshape(tm // CHUNK, CHUNK, tn)


def _mm_res(xs, w_all, layer, res3, mod_exp, k_gate, *, tm=1024, tn=512):
    NB, C, N = res3.shape
    M = NB * C
    tm = _tile(M, tm, C)
    tn = _tile(N, tn)
    n_col = N // tn
    kp = xs[0].shape[1]
    assert all(x.shape == (M, kp) for x in xs) and w_all.shape[1] == kp * len(xs)
    in_specs = [pl.BlockSpec((tm, kp), lambda i, j: (i, 0)) for _ in xs]
    in_specs += [pl.BlockSpec((1, kp, tn), functools.partial(lambda i, j, p: (layer, p, j), p=p))
                 for p in range(len(xs))]
    in_specs += [pl.BlockSpec((tm // C, C, tn), lambda i, j: (i, 0, j)),
                 pl.BlockSpec((tm // C, 1, tn), lambda i, j: (i, 0, k_gate * n_col + j))]
    return pl.pallas_call(
        functools.partial(_mm_res_body, n_pairs=len(xs)),
        grid=(M // tm, n_col),
        in_specs=in_specs,
        out_specs=pl.BlockSpec((tm // C, C, tn), lambda i, j: (i, 0, j)),
        out_shape=jax.ShapeDtypeStruct((NB, C, N), F32),
        compiler_params=_params("arbitrary", "arbitrary"),
        name="out_proj_res",
    )(*xs, *([w_all] * len(xs)), res3, mod_exp)


def _softmax_steps(sts, vt, carry):
    stats, ps = [], []
    for h, st in enumerate(sts):
        m, l = carry[3 * h], carry[3 * h + 1]
        m_new = jnp.maximum(m, jnp.max(st, axis=0, keepdims=True))
        alpha = jnp.exp(m - m_new)
        p = jnp.exp(st - m_new)
        stats.append((m_new, alpha * l + jnp.sum(p, axis=0, keepdims=True), alpha))
        ps.append(p.astype(MXU_DTYPE))
    new = []
    for h, p in enumerate(ps):
        m_new, l, alpha = stats[h]
        vth = vt[h] if isinstance(vt, (list, tuple)) else vt
        new += [m_new, l, alpha * carry[3 * h + 2] + jnp.dot(vth, p, preferred_element_type=F32)]
    return tuple(new)


def _fox_body(q_ref, k_ref, v_ref, o_ref, *, tq, tk, q_off, nq, npp):
    q0 = q_off if nq == 1 else q_off + pl.program_id(2) * tq
    n_full = q0 // tk
    nh = 2 * npp

    def step(kcs, vts, carry, causal):
        sts = [jnp.dot(kcs[h // 2], q_ref[0, h], preferred_element_type=F32) for h in range(nh)]
        if causal:
            row = lax.broadcasted_iota(jnp.int32, sts[0].shape, 0)
            col = lax.broadcasted_iota(jnp.int32, sts[0].shape, 1)
            sts = [jnp.where(row <= col, st, NEG) for st in sts]
        return _softmax_steps(sts, [vts[h // 2] for h in range(nh)], carry)

    def full_chunk(c, carry):
        k0 = pl.multiple_of(c * tk, tk)
        return step([k_ref[0, j, pl.ds(k0, tk), :] for j in range(npp)], [v_ref[0, j, c] for j in range(npp)],
                    carry, False)

    init = (jnp.full((1, tq), NEG, F32), jnp.zeros((1, tq), F32), jnp.zeros((LANES, tq), F32)) * nh
    carry = lax.fori_loop(0, n_full, full_chunk, init)
    k0 = n_full * tk
    k0 = k0 if isinstance(k0, int) else pl.multiple_of(k0, tk)
    carry = step([k_ref[0, j, pl.ds(k0, tq), :] for j in range(npp)],
                 [v_ref[0, j, n_full][:, :tq] for j in range(npp)], carry, True)
    for j in range(npp):
        o0 = carry[6 * j + 2] / carry[6 * j + 1]
        o1 = carry[6 * j + 5] / carry[6 * j + 4]
        o_ref[0, j] = jnp.concatenate([o0[:HEAD_DIM], o1[HEAD_DIM:]], axis=0).astype(o_ref.dtype)


def _split3(x):
    def trunc(v):
        return lax.bitcast_convert_type(lax.bitcast_convert_type(v, jnp.int32) & jnp.int32(-65536), F32)
    hi = trunc(x)
    r1 = x - hi
    mid = trunc(r1)
    return [hi.astype(MXU_DTYPE), mid.astype(MXU_DTYPE), (r1 - mid).astype(MXU_DTYPE)]


def _fox_attention(q, kv, cum_q, cum_k, *, tq, tk):
    B, T, HD = q.shape
    S = kv.shape[1]
    P = HD // LANES
    q_off = S - T
    nq = T // tq
    assert T % tq == 0 and q_off % tk == 0 and (tq == tk or (nq == 1 and tq <= tk))
    nc = -(-S // tk)
    H = 2 * P
    place = np.zeros((3 * H, P * LANES), np.float32)
    ones_row = np.zeros((P * LANES,), np.float32)
    for h in range(H):
        for j in range(3):
            place[j * H + h, (h // 2) * LANES + (h % 2) * 6 + j] = 1.0
            ones_row[(h // 2) * LANES + (h % 2) * 6 + 3 + j] = 1.0
    ck3 = jnp.concatenate(_split3(-cum_k), axis=1)
    k_aug = jnp.einsum('bjs,jc->bsc', ck3, jnp.asarray(place, MXU_DTYPE), preferred_element_type=F32) + ones_row
    k_aug = k_aug.astype(MXU_DTYPE).reshape(B, S, P, LANES)
    kk = jnp.concatenate([kv[..., :HD].reshape(B, S, P, LANES), k_aug], axis=-1).transpose(0, 2, 1, 3)
    qT = q.transpose(0, 2, 1).reshape(B, P, 2, HEAD_DIM, T)
    cq3 = [c.reshape(B, P, 2, 1, T) for c in _split3(cum_q)]
    one = jnp.ones((B, P, 3, T), MXU_DTYPE)
    zrows = lambda n: jnp.zeros((B, P, n, T), MXU_DTYPE)
    heads = []
    for e in range(2):
        rows = [qT[:, :, 0], zrows(HEAD_DIM)] if e == 0 else [zrows(HEAD_DIM), qT[:, :, 1]]
        rows += ([zrows(6 * e)] if e else []) + [one] + [c[:, :, e] for c in cq3] + [zrows(LANES - 6 * e - 6)]
        heads.append(jnp.concatenate(rows, axis=2))
    qt = jnp.stack(heads, axis=2).reshape(B, H, 2 * LANES, T)
    v = jnp.pad(kv[..., HD:], ((0, 0), (0, nc * tk - S), (0, 0)))
    vt = v.reshape(B, nc, tk, P, LANES).transpose(0, 3, 1, 4, 2)
    npp = (P if P <= 8 else 1) if nq == 1 else (2 if P % 2 == 0 else 1)
    ot = pl.pallas_call(
        functools.partial(_fox_body, tq=tq, tk=tk, q_off=q_off, nq=nq, npp=npp),
        grid=(B, P // npp, nq),
        in_specs=[pl.BlockSpec((1, 2 * npp, 2 * LANES, tq), lambda b, p, i: (b, p, 0, i)),
                  pl.BlockSpec((1, npp, S, 2 * LANES), lambda b, p, i: (b, p, 0, 0)),
                  pl.BlockSpec((1, npp, nc, LANES, tk), lambda b, p, i: (b, p, 0, 0, 0))],
        out_specs=pl.BlockSpec((1, npp, LANES, tq), lambda b, p, i: (b, p, 0, i)),
        out_shape=jax.ShapeDtypeStruct((B, P, LANES, T), MXU_DTYPE),
        compiler_params=_params("arbitrary", "arbitrary", "arbitrary"),
        name="fox_attention",
    )(qt, kk, vt)
    return ot.transpose(0, 3, 1, 2).reshape(B, T, HD)


def _dsa_body(qt_ref, k_ref, vt_ref, qit_ref, kid_ref, wit_ref, o_ref, keys_scr, bias_scr, j_scr, *,
              tq, tk, q_off, s_real, topk, n_heads, n_kv, n_idx, nq, n_chunks):
    if nq == 1:
        q0 = q_off
        n_ch = min((q0 + tq + tk - 1) // tk, n_chunks)
        unscanned = float(s_real - min(n_ch * tk, s_real))
    else:
        q0 = q_off + pl.program_id(1) * tq
        n_ch = jnp.minimum((q0 + tq + tk - 1) // tk, n_chunks)
        unscanned = (s_real - jnp.minimum(n_ch * tk, s_real)).astype(F32)
    krow = lax.broadcasted_iota(jnp.int32, (tk, tq), 0)
    qchunk = (q0 + lax.broadcasted_iota(jnp.int32, (tk, tq), 1)) // CHUNK

    def admissible(kpos):
        return ((kpos // CHUNK) <= qchunk) & (kpos < s_real)

    wit = wit_ref[0]

    def score_chunk(c, _):
        kc = kid_ref[0, pl.ds(pl.multiple_of(c * tk, tk), tk), :]
        score = jnp.zeros((tk, tq), F32)
        for h in range(n_idx):
            s = jnp.dot(kc, qit_ref[0, h], preferred_element_type=F32)
            score = score + wit[h:h + 1, :] * jnp.maximum(s, 0.0)
        kpos = c * tk + krow
        sm = jnp.where(admissible(kpos), score, NEG)
        bits = pltpu.bitcast(sm, jnp.int32)
        key = jnp.where(bits < 0, bits ^ 0x7FFFFFFF, bits)
        key = jnp.where(bits == INT_MIN, 0, key)
        keys_scr[c] = jnp.where(kpos < s_real, key, INT_MIN)
        return 0

    lax.fori_loop(0, n_ch, score_chunk, 0)

    if nq == 1:
        if n_ch % 2:
            keys_scr[n_ch] = jnp.full((tk, tq), INT_MIN, jnp.int32)
    else:
        @pl.when(n_ch % 2 == 1)
        def _():
            keys_scr[n_ch] = jnp.full((tk, tq), INT_MIN, jnp.int32)

    def count(pred):
        def body(i, acc):
            for c in (2 * i, 2 * i + 1):
                x = jnp.where(pred(keys_scr[c], c), 1.0, 0.0)
                acc = acc + jnp.sum(x.reshape(tk // 8, 8, tq), axis=0)
            return acc
        acc = lax.fori_loop(0, (n_ch + 1) // 2, body, jnp.zeros((8, tq), F32))
        return jnp.sum(acc, axis=0, keepdims=True)

    kf = float(topk)

    def bit_step(it, carry):
        t, cge = carry
        cand = t + jnp.left_shift(jnp.int32(1), 31 - it)
        cnt = count(lambda kc, c: kc >= cand) + jnp.where(cand <= KEY_NEG, unscanned, 0.0)
        ok = cnt >= kf
        return jnp.where(ok, cand, t), jnp.where(ok, cnt, cge)

    thr, cge = lax.fori_loop(0, 32, bit_step,
                             (jnp.full((1, tq), INT_MIN, jnp.int32), jnp.full((1, tq), 3e38, F32)))
    cgt = count(lambda kc, c: kc > thr) + jnp.where(thr < KEY_NEG, unscanned, 0.0)
    need = kf - cgt

    n_bits = int(n_chunks * tk).bit_length()
    j_scr[...] = jnp.full((1, tq), 2 ** n_bits, jnp.int32)

    @pl.when(jnp.max(cge) > kf)
    def _():
        def j_step(it, jcur):
            cand = jcur + jnp.left_shift(jnp.int32(1), n_bits - 1 - it)
            f = count(lambda kc, c: (kc == thr) & ((c * tk + krow) < cand))
            return jnp.where(f < need, cand, jcur)
        j_scr[...] = lax.fori_loop(0, n_bits, j_step, jnp.zeros((1, tq), jnp.int32))

    jmax = j_scr[...]

    def bias_chunk(c, _):
        kc = keys_scr[c]
        kpos = c * tk + krow
        sel = (kc > thr) | ((kc == thr) & (kpos <= jmax))
        bias_scr[c] = jnp.where(sel & admissible(kpos), 0.0, NEG)
        return 0

    lax.fori_loop(0, n_ch, bias_chunk, 0)

    G2 = 2 * n_heads // n_kv
    outs = []
    for r in range(n_kv // 2):

        def attend(c, carry, r=r):
            kc = k_ref[0, pl.ds(pl.multiple_of(c * tk, tk), tk), r * LANES:(r + 1) * LANES]
            vt = vt_ref[0, r, c]
            b = bias_scr[c]
            sts = [jnp.dot(kc, qt_ref[0, r * G2 + jh], preferred_element_type=F32) + b for jh in range(G2)]
            return _softmax_steps(sts, vt, carry)

        init = (jnp.full((1, tq), NEG, F32), jnp.zeros((1, tq), F32), jnp.zeros((LANES, tq), F32)) * G2
        carry = lax.fori_loop(0, n_ch, attend, init)
        for jh in range(G2):
            kh = (2 * jh) // G2
            outs.append((carry[3 * jh + 2] / carry[3 * jh + 1])[kh * HEAD_DIM:(kh + 1) * HEAD_DIM])
    o_ref[0] = jnp.concatenate(outs, axis=0).astype(o_ref.dtype)


def _dsa_attention(q, k, v, qi, kid, wi, *, s_real, topk, tq, tk):
    B, T, HD = q.shape
    s_pad = k.shape[1]
    n_heads, n_kv, n_idx = HD // HEAD_DIM, k.shape[2] // HEAD_DIM, qi.shape[2] // HEAD_DIM
    assert n_kv % 2 == 0 and n_heads % n_kv == 0 and s_pad % tk == 0 and T % tq == 0
    G = n_heads // n_kv
    nq = T // tq
    nc = s_pad // tk
    qh = q.transpose(0, 2, 1).reshape(B, n_heads, HEAD_DIM, T)
    upper = ((np.arange(n_heads) // G) % 2 == 1)[None, :, None, None]
    zq = jnp.zeros_like(qh)
    qt = jnp.concatenate([jnp.where(upper, zq, qh), jnp.where(upper, qh, zq)], axis=2)
    qih = qi.transpose(0, 2, 1).reshape(B, n_idx, HEAD_DIM, T)
    qit = jnp.concatenate([qih, jnp.zeros_like(qih)], axis=2)
    wit = wi.transpose(0, 2, 1)
    vt = v.reshape(B, nc, tk, n_kv // 2, LANES).transpose(0, 3, 1, 4, 2)
    body = functools.partial(_dsa_body, tq=tq, tk=tk, q_off=s_real - T, s_real=s_real, topk=topk,
                             n_heads=n_heads, n_kv=n_kv, n_idx=n_idx, nq=nq, n_chunks=nc)
    ot = pl.pallas_call(
        body,
        grid=(B, nq),
        in_specs=[pl.BlockSpec((1, n_heads, LANES, tq), lambda b, i: (b, 0, 0, i)),
                  pl.BlockSpec((1, s_pad, k.shape[2]), lambda b, i: (b, 0, 0)),
                  pl.BlockSpec((1, n_kv // 2, nc, LANES, tk), lambda b, i: (b, 0, 0, 0, 0)),
                  pl.BlockSpec((1, n_idx, LANES, tq), lambda b, i: (b, 0, 0, i)),
                  pl.BlockSpec((1, s_pad, LANES), lambda b, i: (b, 0, 0)),
                  pl.BlockSpec((1, n_idx, tq), lambda b, i: (b, 0, i))],
        out_specs=pl.BlockSpec((1, HD, tq), lambda b, i: (b, 0, i)),
        out_shape=jax.ShapeDtypeStruct((B, HD, T), MXU_DTYPE),
        scratch_shapes=[pltpu.VMEM((nc + 1, tk, tq), jnp.int32),
                        pltpu.VMEM((nc, tk, tq), F32),
                        pltpu.VMEM((1, tq), jnp.int32)],
        compiler_params=_params("arbitrary", "arbitrary"),
        name="dsa_attention",
    )(qt, k, vt, qit, kid, wit)
    return ot.transpose(0, 2, 1)


def _swa_body(sinks_ref, q_ref, k0_ref, k1_ref, k2_ref, v0_ref, v1_ref, v2_ref, o_ref, *,
              first_valid, n_heads, n_kv):
    c = pl.program_id(1)
    q = q_ref[0]
    k = jnp.concatenate([k0_ref[0], k1_ref[0], k2_ref[0]], axis=0)
    v = jnp.concatenate([v0_ref[0], v1_ref[0], v2_ref[0]], axis=0)
    n_keys = k.shape[0]
    G = n_heads // n_kv
    valid = (c * CHUNK + lax.broadcasted_iota(jnp.int32, (G * CHUNK, n_keys), 1)) >= first_valid
    ss = []
    for g in range(n_kv):
        qs = jnp.concatenate([q[:, h * HEAD_DIM:(h + 1) * HEAD_DIM] for h in range(g * G, (g + 1) * G)], axis=0)
        ss.append(lax.dot_general(qs, k[:, g * HEAD_DIM:(g + 1) * HEAD_DIM], _NT, preferred_element_type=F32))
    ps = []
    for g in range(n_kv):
        sink = jnp.concatenate([jnp.full((CHUNK, 1), sinks_ref[h], F32) for h in range(g * G, (g + 1) * G)], axis=0)
        s = jnp.where(valid, ss[g], NEG)
        m = jnp.maximum(jnp.max(s, axis=1, keepdims=True), sink)
        e = jnp.exp(s - m)
        ps.append((e / (jnp.sum(e, axis=1, keepdims=True) + jnp.exp(sink - m))).astype(MXU_DTYPE))
    pieces = []
    for g in range(n_kv):
        o = jnp.dot(ps[g], v[:, g * HEAD_DIM:(g + 1) * HEAD_DIM], preferred_element_type=F32)
        pieces += [o[jh * CHUNK:(jh + 1) * CHUNK] for jh in range(G)]
    o_ref[0] = jnp.concatenate(pieces, axis=1).astype(o_ref.dtype)


def _swa_attention(q, kpad, vpad, sinks, *, first_valid):
    B, T, HD = q.shape
    KD = kpad.shape[2]
    n_heads, n_kv = HD // HEAD_DIM, KD // HEAD_DIM
    kv_specs = [pl.BlockSpec((1, CHUNK, KD), functools.partial(lambda b, c, s, o: (b, c + o, 0), o=o))
                for o in range(3)]
    return pl.pallas_call(
        functools.partial(_swa_body, first_valid=first_valid, n_heads=n_heads, n_kv=n_kv),
        grid_spec=pltpu.PrefetchScalarGridSpec(
            num_scalar_prefetch=1,
            grid=(B, T // CHUNK),
            in_specs=[pl.BlockSpec((1, CHUNK, HD), lambda b, c, s: (b, c, 0))] + kv_specs + kv_specs,
            out_specs=pl.BlockSpec((1, CHUNK, HD), lambda b, c, s: (b, c, 0))),
        out_shape=jax.ShapeDtypeStruct((B, T, HD), MXU_DTYPE),
        compiler_params=_params("arbitrary", "arbitrary"),
        name="swa_attention",
    )(sinks.astype(F32), q, kpad, kpad, kpad, vpad, vpad, vpad)


def _gmm1_body(be_ref, nu_ref, x_ref, wg_ref, wu_ref, bg_ref, bu_ref, o_ref, wg_s, wu_s):
    m = pl.program_id(1)
    changed = (m == 0) | (be_ref[m] != be_ref[jnp.maximum(m - 1, 0)])

    @pl.when(changed)
    def _():
        wg_s[...] = wg_ref[0, 0].astype(MXU_DTYPE)
        wu_s[...] = wu_ref[0, 0].astype(MXU_DTYPE)

    def compute(nrows):
        x = x_ref[0:nrows, :]
        x = _unpack_pairs(x) if x.dtype == jnp.int32 else x.astype(MXU_DTYPE)
        tf = wg_s.shape[1]
        sb = min(tf, 2 * LANES)
        bg, bu = bg_ref[0, 0], bu_ref[0, 0]
        for c in range(tf // sb):
            sl = slice(c * sb, (c + 1) * sb)
            g = jnp.dot(x, wg_s[:, sl], preferred_element_type=F32) + bg[:, sl]
            u = jnp.dot(x, wu_s[:, sl], preferred_element_type=F32) + bu[:, sl]
            g = jnp.minimum(g, SWIGLU_LIMIT)
            u = jnp.clip(u, -SWIGLU_LIMIT, SWIGLU_LIMIT)
            o_ref[0:nrows, sl] = (g * _sigmoid(SWIGLU_ALPHA * g) * (u + 1.0)).astype(o_ref.dtype)

    _row_block_dispatch(m, nu_ref, o_ref, compute)


def _row_block_dispatch(m, nu_ref, o_ref, compute):
    tm = o_ref.shape[0]
    half = tm // 2
    used = m < nu_ref[0]
    if tm >= 2 * LANES:
        @pl.when(used & (nu_ref[1 + m] > half))
        def _():
            compute(tm)

        @pl.when(used & (nu_ref[1 + m] <= half))
        def _():
            compute(half)
            o_ref[half:, :] = jnp.zeros((tm - half, o_ref.shape[1]), o_ref.dtype)
    else:
        @pl.when(used)
        def _():
            compute(tm)

    @pl.when(jnp.logical_not(used))
    def _():
        o_ref[...] = jnp.zeros(o_ref.shape, o_ref.dtype)


def _gmm2_body(be_ref, nu_ref, a_ref, w_ref, b_ref, o_ref, w_s):
    m = pl.program_id(1)
    changed = (m == 0) | (be_ref[m] != be_ref[jnp.maximum(m - 1, 0)])

    @pl.when(changed)
    def _():
        w_s[...] = w_ref[0, 0].astype(MXU_DTYPE)

    def compute(nrows):
        o_ref[0:nrows, :] = jnp.dot(a_ref[0:nrows, :], w_s[...], preferred_element_type=F32) + b_ref[0, 0]

    _row_block_dispatch(m, nu_ref, o_ref, compute)


def _moe_experts(xs, blk_e, n_used, layer, w_gu, b_gu, w_dn, b_dn, *, tm, tf=1024, tn=1024):
    R, xw = xs.shape
    L, E, D, F2 = w_gu.shape
    Fh = F2 // 2
    tf = _tile(Fh, tf)
    tn = _tile(D, tn)
    nb = R // tm
    nf = Fh // tf

    def xmap(j, m, be, nu):
        return (jnp.minimum(m, nu[0] - 1), 0)

    act = pl.pallas_call(
        _gmm1_body,
        grid_spec=pltpu.PrefetchScalarGridSpec(
            num_scalar_prefetch=2,
            grid=(nf, nb),
            in_specs=[pl.BlockSpec((tm, xw), xmap),
                      pl.BlockSpec((1, 1, D, tf), lambda j, m, be, nu: (layer, be[m], 0, j)),
                      pl.BlockSpec((1, 1, D, tf), lambda j, m, be, nu: (layer, be[m], 0, nf + j)),
                      pl.BlockSpec((1, 1, 1, tf), lambda j, m, be, nu: (layer, be[m], 0, j)),
                      pl.BlockSpec((1, 1, 1, tf), lambda j, m, be, nu: (layer, be[m], 0, nf + j))],
            out_specs=pl.BlockSpec((tm, tf), lambda j, m, be, nu: (m, j)),
            scratch_shapes=[pltpu.VMEM((D, tf), MXU_DTYPE), pltpu.VMEM((D, tf), MXU_DTYPE)]),
        out_shape=jax.ShapeDtypeStruct((R, Fh), MXU_DTYPE),
        compiler_params=_params("arbitrary", "arbitrary"),
        name="moe_gate_up",
    )(blk_e, n_used, xs, w_gu, w_gu, b_gu.reshape(L, E, 1, F2), b_gu.reshape(L, E, 1, F2))

    ys = pl.pallas_call(
        _gmm2_body,
        grid_spec=pltpu.PrefetchScalarGridSpec(
            num_scalar_prefetch=2,
            grid=(D // tn, nb),
            in_specs=[pl.BlockSpec((tm, Fh), xmap),
                      pl.BlockSpec((1, 1, Fh, tn), lambda j, m, be, nu: (layer, be[m], 0, j)),
                      pl.BlockSpec((1, 1, 1, tn), lambda j, m, be, nu: (layer, be[m], 0, j))],
            out_specs=pl.BlockSpec((tm, tn), lambda j, m, be, nu: (m, j)),
            scratch_shapes=[pltpu.VMEM((Fh, tn), MXU_DTYPE)]),
        out_shape=jax.ShapeDtypeStruct((R, D), F32),
        compiler_params=_params("arbitrary", "arbitrary"),
        name="moe_down",
    )(blk_e, n_used, act, w_dn, b_dn.reshape(L, E, 1, D))
    return ys


def _combine_body(y_ref, gate_ref, res_ref, mg_ref, o_ref):
    gate = gate_ref[...]
    y = None
    for k in range(y_ref.shape[0]):
        t = gate[:, k:k + 1] * y_ref[k]
        y = t if y is None else y + t
    tm, d = y.shape
    o_ref[...] = res_ref[...] + mg_ref[...] * y.reshape(tm // CHUNK, CHUNK, d)


def _moe_combine(yk, gate, res3, mod_exp, k_gate, *, tm=256):
    K, N, D = yk.shape
    NB, C, _ = res3.shape
    tm = _tile(N, tm, C)
    return pl.pallas_call(
        _combine_body,
        grid=(N // tm,),
        in_specs=[pl.BlockSpec((K, tm, D), lambda i: (0, i, 0)),
                  pl.BlockSpec((tm, K), lambda i: (i, 0)),
                  pl.BlockSpec((tm // C, C, D), lambda i: (i, 0, 0)),
                  pl.BlockSpec((tm // C, 1, D), lambda i: (i, 0, k_gate))],
        out_specs=pl.BlockSpec((tm // C, C, D), lambda i: (i, 0, 0)),
        out_shape=jax.ShapeDtypeStruct((NB, C, D), F32),
        compiler_params=_params("arbitrary"),
        name="moe_combine",
    )(yk, gate, res3, mod_exp)


def _moe(h, hp, layer, res3, mod_exp, w_router, b_router, w_gu, b_gu, w_dn, b_dn, *, tm):
    N, D = h.shape
    E = w_router.shape[-1]
    wr = jnp.pad(w_router[layer], ((0, 0), (0, LANES - E)))
    br = jnp.pad(b_router[layer], (0, LANES - E)).reshape(1, LANES)
    logits = _mm(h, wr, [F32], tn=LANES, bias=br)[0][:, :E]
    top_logit, top_e = lax.top_k(logits, TOP_K)
    gate = jax.nn.softmax(top_logit, axis=-1)
    e_flat = top_e.reshape(-1)
    onehot = (e_flat[:, None] == jnp.arange(E, dtype=jnp.int32)[None, :]).astype(jnp.int32)
    rank = jnp.take_along_axis(jnp.cumsum(onehot, axis=0) - onehot, e_flat[:, None], axis=1)[:, 0]
    counts = jnp.sum(onehot, axis=0)
    padded = (counts + tm - 1) // tm * tm
    pend = jnp.cumsum(padded)
    dest = (pend - padded)[e_flat] + rank
    n_blocks = -(-N * TOP_K // tm) + E
    src_tok = jnp.zeros((n_blocks * tm,), jnp.int32).at[dest].set(jnp.arange(N * TOP_K, dtype=jnp.int32) // TOP_K)
    blk_start = jnp.arange(n_blocks, dtype=jnp.int32) * tm
    blk_e = jnp.minimum(jnp.sum((pend[None, :] <= blk_start[:, None]).astype(jnp.int32), axis=1), E - 1)
    rows_valid = jnp.clip((pend - padded + counts)[blk_e] - blk_start, 0, tm)
    n_used = jnp.concatenate([(pend[-1] // tm).reshape(1), rows_valid]).astype(jnp.int32)
    xs = hp[src_tok]
    ys = _moe_experts(xs, blk_e, n_used, layer, w_gu, b_gu, w_dn, b_dn, tm=tm)
    yk = ys[dest.reshape(N, TOP_K).T.reshape(-1)].reshape(TOP_K, N, D)
    return _moe_combine(yk, gate, res3, mod_exp, 5)


def kernel(x_prompt, x_sample, c_prompt, c_sample, cache_fox_kv, cache_fox_logf, cache_dsa_kv, cache_dsa_kidx, cache_swa_kv, w_ada, b_ada, g_mix, g_ffn, w_in_ab, b_forget, w_out_ab, w_in_c, sinks_c, w_out_c, w_router, b_router, w_gu, b_gu, w_dn, b_dn, g_final):
    Bp, Tp, D = x_prompt.shape
    Bs, Ts, _ = x_sample.shape
    depth = w_ada.shape[0]
    past_len = cache_fox_kv.shape[2]
    h_fox = cache_fox_kv.shape[4]
    kv_dsa = cache_dsa_kv.shape[4]
    d_idx = cache_dsa_kidx.shape[3]
    window = cache_swa_kv.shape[2]
    kv_swa = cache_swa_kv.shape[4]
    h_swa = sinks_c.shape[1]
    h_dsa = w_out_ab.shape[1] // HEAD_DIM - h_fox
    fox_w, dq_w, dkv_w = h_fox * HEAD_DIM, h_dsa * HEAD_DIM, kv_dsa * HEAD_DIM
    h_idx = (w_in_ab.shape[2] - 3 * fox_w - h_fox - dq_w - 2 * dkv_w - d_idx) // (d_idx + 1)
    assert Bp == 1 and Ts == CHUNK and Tp % CHUNK == 0 and past_len % CHUNK == 0
    assert d_idx == HEAD_DIM and window == 2 * CHUNK and d_idx + h_fox + h_idx <= LANES
    Mp, Ms = Bp * Tp, Bs * Ts
    M = Mp + Ms
    NB = M // CHUNK
    hd_scale = HEAD_DIM ** -0.5

    pos = jnp.concatenate([jnp.arange(Tp, dtype=jnp.int32),
                           jnp.tile(past_len + jnp.arange(Ts, dtype=jnp.int32), Bs)])
    half = HEAD_DIM // 2
    inv = ROPE_THETA ** (-jnp.arange(half, dtype=F32) / half)
    ang = pos.astype(F32)[:, None] * inv
    cos, sin = jnp.cos(ang), jnp.sin(ang)
    rope = (jnp.concatenate([cos, cos, cos, cos], axis=1), jnp.concatenate([-sin, sin, -sin, sin], axis=1))

    n_c = Bp + Bs
    c_all = jnp.pad(jnp.concatenate([c_prompt, c_sample], axis=0), ((0, -n_c % 8), (0, 0)))
    mod = _ada(c_all, w_ada, b_ada)
    blk_row = jnp.concatenate([jnp.zeros((Mp // CHUNK,), jnp.int32),
                               Bp + jnp.arange(Bs, dtype=jnp.int32)])
    x3 = jnp.concatenate([x_prompt.reshape(Mp // CHUNK, CHUNK, D), x_sample.reshape(Ms // CHUNK, CHUNK, D)], axis=0)

    fox_kv_p, fox_kv_s, logf_p, logf_s, dsa_kv_p, dsa_kv_s, kidx_p, kidx_s, swa_p, swa_s = ([] for _ in range(10))
    moe_tm = 512 if M * TOP_K >= 16384 else 64

    for l in range(depth):
        mod_exp = mod[l][blk_row][:, None, :]
        h = _norm_mod(x3, g_mix[l], mod_exp, 0, 1).reshape(M, D)
        j = l // 2
        if l % 2 == 0:
            w = w_in_ab[j]
            offs = np.cumsum([0, fox_w, fox_w, fox_w, h_fox, dq_w, dkv_w, dkv_w, h_idx * d_idx, d_idx, h_idx])
            seg = lambda a, b: w[:, offs[a]:offs[b]].astype(MXU_DTYPE)
            q_fox = _mm(h, seg(0, 1), [MXU_DTYPE], scale=hd_scale)[0]
            fkv32, fkv = _mm(h, seg(1, 3), [F32, MXU_DTYPE])
            q_dsa = _mm(h, seg(4, 5), [MXU_DTYPE], scale=hd_scale, rope=rope, rope_tiles=10 ** 6)[0]
            dkv32 = _mm(h, seg(5, 7), [F32], tn=dkv_w, rope=rope, rope_tiles=1)[0]
            qi = _mm(h, seg(7, 8), [MXU_DTYPE], scale=d_idx ** -0.5, rope=rope, rope_tiles=10 ** 6)[0]
            pad = LANES - d_idx - h_fox - h_idx
            w_small = jnp.concatenate([w[:, offs[8]:offs[9]], w[:, offs[3]:offs[4]], w[:, offs[9]:offs[10]],
                                       jnp.zeros((D, pad), w.dtype)], axis=1).astype(MXU_DTYPE)
            bf_row = jnp.pad(b_forget[j], (d_idx, LANES - d_idx - h_fox)).reshape(1, LANES)
            small = _mm_small(h, w_small, rope, bf_row, d_idx=d_idx, h_fox=h_fox, wi_scale=h_idx ** -0.5)
            kidx, logf, wi = small[:, :d_idx], small[:, d_idx:d_idx + h_fox], small[:, d_idx + h_fox:d_idx + h_fox + h_idx]

            fox_kv_p.append(fkv32[:Mp].reshape(Bp, Tp, 2, h_fox, HEAD_DIM))
            fox_kv_s.append(fkv32[Mp:].reshape(Bs, Ts, 2, h_fox, HEAD_DIM))
            logf_p.append(logf[:Mp].reshape(Bp, Tp, h_fox))
            logf_s.append(logf[Mp:].reshape(Bs, Ts, h_fox))
            dsa_kv_p.append(dkv32[:Mp].reshape(Bp, Tp, 2, kv_dsa, HEAD_DIM))
            dsa_kv_s.append(dkv32[Mp:].reshape(Bs, Ts, 2, kv_dsa, HEAD_DIM))
            kidx_p.append(kidx[:Mp].reshape(Bp, Tp, d_idx))
            kidx_s.append(kidx[Mp:].reshape(Bs, Ts, d_idx))

            cum_p = jnp.cumsum(logf_p[-1].transpose(0, 2, 1), axis=2)
            oa_p = _fox_attention(q_fox[:Mp].reshape(Bp, Tp, fox_w), fkv[:Mp].reshape(Bp, Tp, 2 * fox_w),
                                  cum_p, cum_p, tq=min(512, Tp), tk=min(512, Tp))
            kv_s = jnp.concatenate([cache_fox_kv[j].reshape(Bs, past_len, 2 * fox_w).astype(MXU_DTYPE),
                                    fkv[Mp:].reshape(Bs, Ts, 2 * fox_w)], axis=1)
            cum_s = jnp.cumsum(jnp.concatenate([cache_fox_logf[j].astype(F32), logf_s[-1]], axis=1).transpose(0, 2, 1),
                               axis=2)
            oa_s = _fox_attention(q_fox[Mp:].reshape(Bs, Ts, fox_w), kv_s, cum_s[:, :, past_len:], cum_s,
                                  tq=Ts, tk=_tile(past_len, 256, CHUNK))
            dkv = dkv32.astype(MXU_DTYPE)
            kid = jnp.concatenate([kidx, kidx], axis=1).astype(MXU_DTYPE)
            ob_p = _dsa_attention(q_dsa[:Mp].reshape(Bp, Tp, dq_w), dkv[:Mp, :dkv_w].reshape(Bp, Tp, dkv_w),
                                  dkv[:Mp, dkv_w:].reshape(Bp, Tp, dkv_w), qi[:Mp].reshape(Bp, Tp, -1),
                                  kid[:Mp].reshape(Bp, Tp, LANES), wi[:Mp].reshape(Bp, Tp, h_idx),
                                  s_real=Tp, topk=min(TOPK_MAX, Tp // 4), tq=min(256, Tp), tk=min(512, Tp))
            S = past_len + Ts
            s_pad = -(-S // LANES) * LANES
            padk = lambda a: jnp.pad(a, ((0, 0), (0, s_pad - S), (0, 0)))
            pkv = cache_dsa_kv[j].reshape(Bs, past_len, 2 * dkv_w).astype(MXU_DTYPE)
            k_s = padk(jnp.concatenate([pkv[:, :, :dkv_w], dkv[Mp:, :dkv_w].reshape(Bs, Ts, dkv_w)], axis=1))
            v_s = padk(jnp.concatenate([pkv[:, :, dkv_w:], dkv[Mp:, dkv_w:].reshape(Bs, Ts, dkv_w)], axis=1))
            pki = cache_dsa_kidx[j].astype(MXU_DTYPE)
            kid_s = padk(jnp.concatenate([jnp.concatenate([pki, pki], axis=2), kid[Mp:].reshape(Bs, Ts, LANES)], axis=1))
            ob_s = _dsa_attention(q_dsa[Mp:].reshape(Bs, Ts, dq_w), k_s, v_s, qi[Mp:].reshape(Bs, Ts, -1),
                                  kid_s, wi[Mp:].reshape(Bs, Ts, h_idx),
                                  s_real=S, topk=min(TOPK_MAX, S // 4), tq=Ts, tk=_tile(s_pad, 512))
            oa = jnp.concatenate([oa_p.reshape(Mp, fox_w), oa_s.reshape(Ms, fox_w)], axis=0)
            ob = jnp.concatenate([ob_p.reshape(Mp, dq_w), ob_s.reshape(Ms, dq_w)], axis=0)
            if fox_w == dq_w:
                x3 = _mm_res([oa, ob], w_out_ab, j, x3, mod_exp, 2)
            else:
                x3 = _mm_res([jnp.concatenate([oa, ob], axis=1)], w_out_ab, j, x3, mod_exp, 2)
        else:
            w = w_in_c[j]
            qw, kw = h_swa * HEAD_DIM, kv_swa * HEAD_DIM
            q = _mm(h, w[:, :qw].astype(MXU_DTYPE), [MXU_DTYPE], scale=hd_scale, rope=rope, rope_tiles=10 ** 6)[0]
            kv32 = _mm(h, w[:, qw:].astype(MXU_DTYPE), [F32], tn=kw, rope=rope, rope_tiles=1)[0]
            kvb = kv32.astype(MXU_DTYPE)
            kv_new_p = kv32[:Mp].reshape(Bp, Tp, 2, kv_swa, HEAD_DIM)
            kv_new_s = kv32[Mp:].reshape(Bs, Ts, 2, kv_swa, HEAD_DIM)
            swa_p.append(kv_new_p[:, Tp - min(window, Tp):])
            swa_s.append(jnp.concatenate([cache_swa_kv[j], kv_new_s], axis=1)[:, Ts:])
            zpad = jnp.zeros((Bp, window, kw), MXU_DTYPE)
            o_p = _swa_attention(q[:Mp].reshape(Bp, Tp, qw),
                                 jnp.concatenate([zpad, kvb[:Mp, :kw].reshape(Bp, Tp, kw)], axis=1),
                                 jnp.concatenate([zpad, kvb[:Mp, kw:].reshape(Bp, Tp, kw)], axis=1),
                                 sinks_c[j], first_valid=window)
            past = cache_swa_kv[j].reshape(Bs, window, 2 * kw).astype(MXU_DTYPE)
            o_s = _swa_attention(q[Mp:].reshape(Bs, Ts, qw),
                                 jnp.concatenate([past[:, :, :kw], kvb[Mp:, :kw].reshape(Bs, Ts, kw)], axis=1),
                                 jnp.concatenate([past[:, :, kw:], kvb[Mp:, kw:].reshape(Bs, Ts, kw)], axis=1),
                                 sinks_c[j], first_valid=0)
            o = jnp.concatenate([o_p.reshape(Mp, qw), o_s.reshape(Ms, qw)], axis=0)
            x3 = _mm_res([o], w_out_c, j, x3, mod_exp, 2)

        h2, h2p = _norm_mod(x3, g_ffn[l], mod_exp, 3, 4, packed=True)
        x3 = _moe(h2.reshape(M, D), h2p.reshape(M, -1), l, x3, mod_exp,
                  w_router, b_router, w_gu, b_gu, w_dn, b_dn, tm=moe_tm)

    y = _final_norm(x3, g_final).reshape(M, D)
    st = lambda xs: jnp.stack(xs, axis=0)
    return (y[:Mp].reshape(Bp, Tp, D), y[Mp:].reshape(Bs, Ts, D),
            st(fox_kv_p), st(fox_kv_s), st(logf_p), st(logf_s), st(dsa_kv_p), st(dsa_kv_s),
            st(kidx_p), st(kidx_s), st(swa_p), st(swa_s))
```
